```python
import jax, jax.numpy as jnp
from jax import lax
import numpy as np

D_MODEL = 1024
BATCH = 16
SEQ = 2048
DEPTH = 1
DEC_BATCH = 32
DEC_SEQ = 4
PAST_LEN = 16384
PAGE_SIZE = 128

HEAD_DIM = 64
NSA_HEADS = 8
NSA_KV_HEADS = 2
NSA_GROUP = NSA_HEADS // NSA_KV_HEADS
CMP_BLOCK = 32
CMP_STRIDE = 16
CMP_HIDDEN = 2 * HEAD_DIM
SEL_BLOCK = 64
N_SEL = 16
WINDOW = 512
FOX_HEADS = 8
D_FF = 3 * D_MODEL
CONV_W = 3
Q_BLOCK = 128
SEL_Q_BLOCK = 32
D_QA = NSA_HEADS * HEAD_DIM
D_KVA = NSA_KV_HEADS * HEAD_DIM
D_FOX = FOX_HEADS * HEAD_DIM
IN_SIZES = (D_QA, D_KVA, D_KVA, D_KVA, D_KVA, D_KVA, D_KVA, 3 * NSA_HEADS, D_FOX, D_FOX, D_FOX, FOX_HEADS, D_MODEL, D_MODEL)
D_IN = sum(IN_SIZES)
ATTN_SCALE = HEAD_DIM ** -0.5
FORGET_BIAS = 3.0
FORCED_SCORE = 1e6
NEG_INF = -1e30
TINY = 1e-30
EPS = 1e-6

kernel_name = 'nsa_fox_parallel_convffn_step'

WEIGHT_NAMES = ('norm_attn_g', 'w_in', 'b_forget', 'q_norm_a_g', 'k_norm_cmp_g', 'k_norm_sel_g', 'k_norm_win_g',
                'cmp_pe_k', 'cmp_w1_k', 'cmp_b1_k', 'cmp_w2_k', 'cmp_pe_v', 'cmp_w1_v', 'cmp_b1_v', 'cmp_w2_v',
                'q_norm_b_g', 'k_norm_b_g', 'w_o_a', 'w_o_b', 'w_out', 'norm_ffn_g', 'w_up', 'w_gate', 'conv_w', 'conv_b', 'w_down')


def rms_norm(x, g):
    xf = x.astype(jnp.float32)
    y = xf * lax.rsqrt(jnp.mean(xf * xf, axis=-1, keepdims=True) + EPS)
    return (y * g.astype(jnp.float32)).astype(x.dtype)


def masked_softmax(logits, mask):
    logits = jnp.where(mask, logits.astype(jnp.float32), NEG_INF)
    m = jnp.max(logits, axis=-1, keepdims=True)
    e = jnp.where(mask, jnp.exp(logits - m), 0.0)
    return e / jnp.maximum(jnp.sum(e, axis=-1, keepdims=True), TINY)


def alibi_slopes():
    h = jnp.arange(1, NSA_HEADS + 1, dtype=jnp.float32)
    return jnp.exp2(-8.0 * h / NSA_HEADS).reshape(NSA_KV_HEADS, NSA_GROUP)


def gather_pages(pool, page_table):
    rows = pool[page_table]
    return rows.reshape((rows.shape[0], rows.shape[1] * rows.shape[2]) + rows.shape[3:])


def project(xn, p):
    B, T, _ = xn.shape
    z = jnp.einsum('btd,de->bte', xn, p['w_in'])
    offs = np.cumsum(IN_SIZES)[:-1].tolist()
    (qa, ck, cv, sk, sv, wk, wv, ng, fq, fk, fv, ff, ga, gb) = jnp.split(z, offs, axis=-1)
    heads = lambda t, h: t.reshape(B, T, h, HEAD_DIM)
    qa = rms_norm(heads(qa, NSA_HEADS), p['q_norm_a_g'])
    ck, cv = heads(ck, NSA_KV_HEADS), heads(cv, NSA_KV_HEADS)
    sk, sv = rms_norm(heads(sk, NSA_KV_HEADS), p['k_norm_sel_g']), heads(sv, NSA_KV_HEADS)
    wk, wv = rms_norm(heads(wk, NSA_KV_HEADS), p['k_norm_win_g']), heads(wv, NSA_KV_HEADS)
    ng = jax.nn.sigmoid(ng.reshape(B, T, NSA_HEADS, 3))
    fq = rms_norm(heads(fq, FOX_HEADS), p['q_norm_b_g'])
    fk = rms_norm(heads(fk, FOX_HEADS), p['k_norm_b_g'])
    fv = heads(fv, FOX_HEADS)
    logf = jax.nn.log_sigmoid(ff.astype(jnp.float32) + p['b_forget'].astype(jnp.float32))
    return (qa, ck, cv, sk, sv, wk, wv, ng, fq, fk, fv, logf, jax.nn.sigmoid(ga), jax.nn.sigmoid(gb))


def compress_rows(rows, pe, w1, b1, w2):
    B, L = rows.shape[:2]
    nc = L // CMP_STRIDE
    chunks = rows[:, :nc * CMP_STRIDE].reshape(B, nc, CMP_STRIDE, NSA_KV_HEADS, HEAD_DIM)
    lo = jnp.einsum('bcpgd,pdh->bcgh', chunks + pe[:CMP_STRIDE, None, :], w1[:CMP_STRIDE])
    hi = jnp.einsum('bcpgd,pdh->bcgh', chunks + pe[CMP_STRIDE:, None, :], w1[CMP_STRIDE:])
    h = jax.nn.gelu(lo[:, :-1] + hi[:, 1:] + b1)
    return jnp.einsum('bcgh,he->bcge', h, w2)


def nsa_cmp_sel(q, qpos, ck_raw, cv_raw, sk, sv, p):
    B, T = q.shape[:2]
    L = ck_raw.shape[1]
    sl = alibi_slopes()
    qg = q.reshape(B, T, NSA_KV_HEADS, NSA_GROUP, HEAD_DIM)
    kc = rms_norm(compress_rows(ck_raw, p['cmp_pe_k'], p['cmp_w1_k'], p['cmp_b1_k'], p['cmp_w2_k']), p['k_norm_cmp_g'])
    vc = compress_rows(cv_raw, p['cmp_pe_v'], p['cmp_w1_v'], p['cmp_b1_v'], p['cmp_w2_v'])
    n_cmp = kc.shape[1]
    c_start = jnp.arange(n_cmp, dtype=jnp.int32) * CMP_STRIDE
    dist_c = qpos[:, None] - (c_start + CMP_BLOCK - 1)[None, :]
    logits = (jnp.einsum('btgrd,bcgd->bgrtc', qg, kc).astype(jnp.float32) * ATTN_SCALE
              - sl[None, :, :, None, None] * dist_c.astype(jnp.float32))
    p_cmp = masked_softmax(logits, dist_c >= 0)
    o_cmp = jnp.einsum('bgrtc,bcgd->btgrd', p_cmp.astype(vc.dtype), vc).reshape(B, T, NSA_HEADS, HEAD_DIM)
    n_slc = -(-L // SEL_BLOCK)
    s_start = jnp.arange(n_slc, dtype=jnp.int32) * SEL_BLOCK
    overlap = ((c_start[:, None] < s_start[None, :] + SEL_BLOCK) & (c_start[:, None] + CMP_BLOCK > s_start[None, :])).astype(jnp.float32)
    p_slc = jnp.einsum('bgrtc,cj->bgtj', p_cmp, overlap)
    blk = jnp.arange(n_slc, dtype=jnp.int32)[None, :]
    cur = (qpos // SEL_BLOCK)[:, None]
    forced = (blk == 0) | (blk == cur) | (blk == cur - 1)
    avail = s_start[None, :] <= qpos[:, None]
    score = jnp.where(forced, FORCED_SCORE, jnp.where(avail, p_slc, -1.0))
    k_top = min(N_SEL, n_slc)
    _, idx = lax.top_k(score, k_top)
    pad = ((0, 0), (0, n_slc * SEL_BLOCK - L), (0, 0), (0, 0))
    kb = jnp.pad(sk, pad).reshape(B, n_slc, SEL_BLOCK, NSA_KV_HEADS, HEAD_DIM).transpose(0, 3, 1, 2, 4)
    vb = jnp.pad(sv, pad).reshape(B, n_slc, SEL_BLOCK, NSA_KV_HEADS, HEAD_DIM).transpose(0, 3, 1, 2, 4)
    qb = SEL_Q_BLOCK if T % SEL_Q_BLOCK == 0 else T
    nq = T // qb
    bi = jnp.arange(B)[:, None, None, None]
    gi = jnp.arange(NSA_KV_HEADS)[None, :, None, None]
    n_keys = k_top * SEL_BLOCK

    def sel_block(args):
        q_c, pos_c, idx_c = args
        kg = kb[bi, gi, idx_c]
        vg = vb[bi, gi, idx_c]
        kpos = idx_c[..., None] * SEL_BLOCK + jnp.arange(SEL_BLOCK, dtype=jnp.int32)
        dist = pos_c[None, None, :, None, None] - kpos
        lg = (jnp.einsum('bqgrd,bgqksd->bgrqks', q_c, kg).astype(jnp.float32) * ATTN_SCALE
              - sl[None, :, :, None, None, None] * dist[:, :, None].astype(jnp.float32))
        pr = masked_softmax(lg.reshape(B, NSA_KV_HEADS, NSA_GROUP, qb, n_keys),
                            (dist >= 0).reshape(B, NSA_KV_HEADS, 1, qb, n_keys))
        o = jnp.einsum('bgrqn,bgqnd->bqgrd', pr.astype(vg.dtype), vg.reshape(B, NSA_KV_HEADS, qb, n_keys, HEAD_DIM))
        return o.reshape(B, qb, NSA_HEADS, HEAD_DIM)

    q_chunks = qg.reshape(B, nq, qb, NSA_KV_HEADS, NSA_GROUP, HEAD_DIM).transpose(1, 0, 2, 3, 4, 5)
    idx_chunks = idx.reshape(B, NSA_KV_HEADS, nq, qb, k_top).transpose(2, 0, 1, 3, 4)
    o_sel = lax.map(sel_block, (q_chunks, qpos.reshape(nq, qb), idx_chunks))
    o_sel = jnp.swapaxes(o_sel, 0, 1).reshape(B, T, NSA_HEADS, HEAD_DIM)
    return o_cmp, o_sel


def window_attend(q, qpos, k, v, kpos):
    B, T = q.shape[:2]
    qg = q.reshape(B, T, NSA_KV_HEADS, NSA_GROUP, HEAD_DIM)
    dist = qpos[:, None] - kpos[None, :]
    mask = (dist >= 0) & (dist <= WINDOW) & (kpos[None, :] >= 0)
    logits = (jnp.einsum('btgrd,bsgd->bgrts', qg, k).astype(jnp.float32) * ATTN_SCALE
              - alibi_slopes()[None, :, :, None, None] * dist.astype(jnp.float32))
    pr = masked_softmax(logits, mask)
    o = jnp.einsum('bgrts,bsgd->btgrd', pr.astype(v.dtype), v)
    return o.reshape(B, T, NSA_HEADS, HEAD_DIM)


def window_prompt(q, k, v):
    B, T = q.shape[:2]
    pad = ((0, 0), (WINDOW, 0), (0, 0), (0, 0))
    kp, vp = jnp.pad(k, pad), jnp.pad(v, pad)
    span = Q_BLOCK + WINDOW

    def block(i):
        start = i * Q_BLOCK
        q_c = lax.dynamic_slice_in_dim(q, start, Q_BLOCK, axis=1)
        k_c = lax.dynamic_slice_in_dim(kp, start, span, axis=1)
        v_c = lax.dynamic_slice_in_dim(vp, start, span, axis=1)
        qpos = start + jnp.arange(Q_BLOCK, dtype=jnp.int32)
        kpos = start - WINDOW + jnp.arange(span, dtype=jnp.int32)
        return window_attend(q_c, qpos, k_c, v_c, kpos)

    o = lax.map(block, jnp.arange(T // Q_BLOCK, dtype=jnp.int32))
    return jnp.swapaxes(o, 0, 1).reshape(B, T, NSA_HEADS, HEAD_DIM)


def fox_attend(q, qpos, cq, segs):
    cq_t = jnp.swapaxes(cq, 1, 2)[:, :, :, None]
    logits = jnp.concatenate(
        [jnp.einsum('bthd,bshd->bhts', q, k).astype(jnp.float32) * ATTN_SCALE
         + cq_t - jnp.swapaxes(ck, 1, 2)[:, :, None, :] for k, _, _, ck in segs], axis=-1)
    mask = jnp.concatenate([kpos[None, :] <= qpos[:, None] for _, _, kpos, _ in segs], axis=-1)
    pr = masked_softmax(logits, mask)
    out, start = 0.0, 0
    for _, v, _, _ in segs:
        n = v.shape[1]
        out = out + jnp.einsum('bhts,bshd->bthd', pr[..., start:start + n].astype(v.dtype), v)
        start += n
    return out


def fox_prompt(q, k, v, c):
    B, T = q.shape[:2]
    kpos = jnp.arange(T, dtype=jnp.int32)

    def block(i):
        start = i * Q_BLOCK
        q_c = lax.dynamic_slice_in_dim(q, start, Q_BLOCK, axis=1)
        c_c = lax.dynamic_slice_in_dim(c, start, Q_BLOCK, axis=1)
        return fox_attend(q_c, start + jnp.arange(Q_BLOCK, dtype=jnp.int32), c_c, ((k, v, kpos, c),))

    o = lax.map(block, jnp.arange(T // Q_BLOCK, dtype=jnp.int32))
    return jnp.swapaxes(o, 0, 1).reshape(B, T, FOX_HEADS, HEAD_DIM)


def merge_branches(x, o_cmp, o_sel, o_win, ng, o_fox, ga, gb, p):
    B, T = x.shape[:2]
    o_a = ng[..., 0:1] * o_cmp + ng[..., 1:2] * o_sel + ng[..., 2:3] * o_win
    ya = jnp.einsum('bte,ed->btd', o_a.reshape(B, T, D_QA), p['w_o_a'])
    yb = jnp.einsum('bte,ed->btd', o_fox.reshape(B, T, D_FOX), p['w_o_b'])
    return x + jnp.einsum('bte,ed->btd', ga * ya + gb * yb, p['w_out'])


def conv_ffn(h, prev_rows, p):
    hn = rms_norm(h, p['norm_ffn_g'])
    u = jnp.einsum('btd,df->btf', hn, p['w_up'])
    g = jnp.einsum('btd,df->btf', hn, p['w_gate'])
    T = u.shape[1]
    ue = jnp.concatenate([prev_rows.astype(u.dtype), u], axis=1)
    uc = p['conv_b'] + sum(p['conv_w'][k] * ue[:, k:k + T] for k in range(CONV_W))
    y = jnp.einsum('btf,fd->btd', jax.nn.gelu(uc) * g, p['w_down'])
    return h + y, ue[:, -(CONV_W - 1):]


def prompt_layer(x, p):
    B, T, _ = x.shape
    pos = jnp.arange(T, dtype=jnp.int32)
    xn = rms_norm(x, p['norm_attn_g'])
    (qa, ck, cv, sk, sv, wk, wv, ng, fq, fk, fv, logf, ga, gb) = project(xn, p)
    o_cmp, o_sel = nsa_cmp_sel(qa, pos, ck, cv, sk, sv, p)
    o_win = window_prompt(qa, wk, wv)
    o_fox = fox_prompt(fq, fk, fv, jnp.cumsum(logf, axis=1))
    h = merge_branches(x, o_cmp, o_sel, o_win, ng, o_fox, ga, gb, p)
    y, conv_rows = conv_ffn(h, jnp.zeros((B, CONV_W - 1, D_FF), x.dtype), p)
    w_keep = min(WINDOW, T)
    return y, (ck, cv, sk, sv, fk, fv, logf, wk[:, T - w_keep:], wv[:, T - w_keep:], conv_rows)


def sample_layer(x, cache, page_table, p):
    (c_cmp_k, c_cmp_v, c_sel_k, c_sel_v, c_fox_k, c_fox_v, c_fox_logf, s_win_k, s_win_v, s_conv) = cache
    B, T, _ = x.shape
    past = page_table.shape[1] * PAGE_SIZE
    qpos = past + jnp.arange(T, dtype=jnp.int32)
    xn = rms_norm(x, p['norm_attn_g'])
    (qa, ck, cv, sk, sv, wk, wv, ng, fq, fk, fv, logf, ga, gb) = project(xn, p)
    cat = lambda pool, new: jnp.concatenate([gather_pages(pool, page_table), new.astype(pool.dtype)], axis=1)
    o_cmp, o_sel = nsa_cmp_sel(qa, qpos, cat(c_cmp_k, ck), cat(c_cmp_v, cv), cat(c_sel_k, sk), cat(c_sel_v, sv), p)
    w_buf = s_win_k.shape[1]
    wk_all = jnp.concatenate([s_win_k, wk.astype(s_win_k.dtype)], axis=1)
    wv_all = jnp.concatenate([s_win_v, wv.astype(s_win_v.dtype)], axis=1)
    kpos_w = past - w_buf + jnp.arange(w_buf + T, dtype=jnp.int32)
    o_win = window_attend(qa, qpos, wk_all, wv_all, kpos_w)
    lf_past = gather_pages(c_fox_logf, page_table).astype(jnp.float32)
    c_past = jnp.cumsum(lf_past, axis=1) - jnp.sum(lf_past, axis=1, keepdims=True)
    c_new = jnp.cumsum(logf, axis=1)
    o_fox = fox_attend(fq, qpos, c_new,
                       ((gather_pages(c_fox_k, page_table), gather_pages(c_fox_v, page_table),
                         jnp.arange(past, dtype=jnp.int32), c_past),
                        (fk, fv, qpos, c_new)))
    h = merge_branches(x, o_cmp, o_sel, o_win, ng, o_fox, ga, gb, p)
    y, conv_rows = conv_ffn(h, s_conv, p)
    return y, (ck, cv, sk, sv, fk, fv, logf, wk_all[:, -w_buf:], wv_all[:, -w_buf:], conv_rows)


def setup_inputs(seed: int = 0) -> dict:
    key = jax.random.key(seed)
    ks = iter(jax.random.split(key, 64))
    nrm = lambda shape, scale: scale * jax.random.normal(next(ks), shape, jnp.float32)
    gain = lambda shape: 1.0 + 0.01 * jax.random.normal(next(ks), shape, jnp.float32)
    n_pages = PAST_LEN // PAGE_SIZE
    n_pool = (DEC_BATCH * n_pages * 5) // 4
    w_buf = min(WINDOW, PAST_LEN)
    Ld = DEPTH
    kv_pool = (Ld, n_pool, PAGE_SIZE, NSA_KV_HEADS, HEAD_DIM)
    fox_pool = (Ld, n_pool, PAGE_SIZE, FOX_HEADS, HEAD_DIM)
    page_table = jax.random.permutation(next(ks), n_pool)[:DEC_BATCH * n_pages].reshape(DEC_BATCH, n_pages).astype(jnp.int32)
    return {
        'x_prompt': nrm((BATCH, SEQ, D_MODEL), 1.0),
        'x_sample': nrm((DEC_BATCH, DEC_SEQ, D_MODEL), 1.0),
        'cache_cmp_k': nrm(kv_pool, 1.0),
        'cache_cmp_v': nrm(kv_pool, 1.0),
        'cache_sel_k': nrm(kv_pool, 1.0),
        'cache_sel_v': nrm(kv_pool, 1.0),
        'cache_fox_k': nrm(fox_pool, 1.0),
        'cache_fox_v': nrm(fox_pool, 1.0),
        'cache_fox_logf': jax.nn.log_sigmoid(FORGET_BIAS + nrm((Ld, n_pool, PAGE_SIZE, FOX_HEADS), 1.0)),
        'state_win_k': nrm((Ld, DEC_BATCH, w_buf, NSA_KV_HEADS, HEAD_DIM), 1.0),
        'state_win_v': nrm((Ld, DEC_BATCH, w_buf, NSA_KV_HEADS, HEAD_DIM), 1.0),
        'state_ffn_conv': nrm((Ld, DEC_BATCH, CONV_W - 1, D_FF), 1.0),
        'page_table': page_table,
        'norm_attn_g': gain((Ld, D_MODEL)),
        'w_in': nrm((Ld, D_MODEL, D_IN), D_MODEL ** -0.5),
        'b_forget': FORGET_BIAS + nrm((Ld, FOX_HEADS), 0.1),
        'q_norm_a_g': gain((Ld, HEAD_DIM)),
        'k_norm_cmp_g': gain((Ld, HEAD_DIM)),
        'k_norm_sel_g': gain((Ld, HEAD_DIM)),
        'k_norm_win_g': gain((Ld, HEAD_DIM)),
        'cmp_pe_k': nrm((Ld, CMP_BLOCK, HEAD_DIM), 0.02),
        'cmp_w1_k': nrm((Ld, CMP_BLOCK, HEAD_DIM, CMP_HIDDEN), (CMP_BLOCK * HEAD_DIM) ** -0.5),
        'cmp_b1_k': nrm((Ld, CMP_HIDDEN), 0.01),
        'cmp_w2_k': nrm((Ld, CMP_HIDDEN, HEAD_DIM), CMP_HIDDEN ** -0.5),
        'cmp_pe_v': nrm((Ld, CMP_BLOCK, HEAD_DIM), 0.02),
        'cmp_w1_v': nrm((Ld, CMP_BLOCK, HEAD_DIM, CMP_HIDDEN), (CMP_BLOCK * HEAD_DIM) ** -0.5),
        'cmp_b1_v': nrm((Ld, CMP_HIDDEN), 0.01),
        'cmp_w2_v': nrm((Ld, CMP_HIDDEN, HEAD_DIM), CMP_HIDDEN ** -0.5),
        'q_norm_b_g': gain((Ld, HEAD_DIM)),
        'k_norm_b_g': gain((Ld, HEAD_DIM)),
        'w_o_a': nrm((Ld, D_QA, D_MODEL), D_QA ** -0.5),
        'w_o_b': nrm((Ld, D_FOX, D_MODEL), D_FOX ** -0.5),
        'w_out': nrm((Ld, D_MODEL, D_MODEL), D_MODEL ** -0.5),
        'norm_ffn_g': gain((Ld, D_MODEL)),
        'w_up': nrm((Ld, D_MODEL, D_FF), D_MODEL ** -0.5),
        'w_gate': nrm((Ld, D_MODEL, D_FF), D_MODEL ** -0.5),
        'conv_w': nrm((Ld, CONV_W, D_FF), CONV_W ** -0.5),
        'conv_b': nrm((Ld, D_FF), 0.01),
        'w_down': nrm((Ld, D_FF, D_MODEL), D_FF ** -0.5),
    }


def reference(x_prompt, x_sample, cache_cmp_k, cache_cmp_v, cache_sel_k, cache_sel_v, cache_fox_k, cache_fox_v,
              cache_fox_logf, state_win_k, state_win_v, state_ffn_conv, page_table,
              norm_attn_g, w_in, b_forget, q_norm_a_g, k_norm_cmp_g, k_norm_sel_g, k_norm_win_g,
              cmp_pe_k, cmp_w1_k, cmp_b1_k, cmp_w2_k, cmp_pe_v, cmp_w1_v, cmp_b1_v, cmp_w2_v,
              q_norm_b_g, k_norm_b_g, w_o_a, w_o_b, w_out, norm_ffn_g, w_up, w_gate, conv_w, conv_b, w_down):
    weights = (norm_attn_g, w_in, b_forget, q_norm_a_g, k_norm_cmp_g, k_norm_sel_g, k_norm_win_g,
               cmp_pe_k, cmp_w1_k, cmp_b1_k, cmp_w2_k, cmp_pe_v, cmp_w1_v, cmp_b1_v, cmp_w2_v,
               q_norm_b_g, k_norm_b_g, w_o_a, w_o_b, w_out, norm_ffn_g, w_up, w_gate, conv_w, conv_b, w_down)
    caches = (cache_cmp_k, cache_cmp_v, cache_sel_k, cache_sel_v, cache_fox_k, cache_fox_v, cache_fox_logf,
              state_win_k, state_win_v, state_ffn_conv)
    prompt_states, sample_states = [], []
    for l in range(DEPTH):
        p = {name: w[l] for name, w in zip(WEIGHT_NAMES, weights)}
        x_prompt, st_p = prompt_layer(x_prompt, p)
        x_sample, st_s = sample_layer(x_sample, tuple(c[l] for c in caches), page_table, p)
        prompt_states.append(st_p)
        sample_states.append(st_s)
    new_prompt = [jnp.stack(rows) for rows in zip(*prompt_states)]
    new_sample = [jnp.stack(rows) for rows in zip(*sample_states)]
    return (x_prompt, x_sample, *new_prompt, *new_sample)
```

```python
import functools

import numpy as np
import jax
import jax.numpy as jnp
from jax import lax
from jax.experimental import pallas as pl
from jax.experimental.pallas import tpu as pltpu

F32 = jnp.float32
BF16 = jnp.bfloat16

D_MODEL = 1024
HEAD_DIM = 64
NSA_HEADS = 8
NSA_KV_HEADS = 2
NSA_GROUP = NSA_HEADS // NSA_KV_HEADS
FOX_HEADS = 8
CMP_BLOCK = 32
CMP_STRIDE = 16
CMP_HIDDEN = 2 * HEAD_DIM
SEL_BLOCK = 64
N_SEL = 16
WINDOW = 512
D_FF = 3 * D_MODEL
CONV_W = 3
PAGE_SIZE = 128
D_QA = NSA_HEADS * HEAD_DIM
D_KVA = NSA_KV_HEADS * HEAD_DIM
D_FOX = FOX_HEADS * HEAD_DIM
IN_SIZES = (D_QA, D_KVA, D_KVA, D_KVA, D_KVA, D_KVA, D_KVA, 3 * NSA_HEADS, D_FOX, D_FOX, D_FOX, FOX_HEADS,
            D_MODEL, D_MODEL)
ATTN_SCALE = HEAD_DIM ** -0.5
FORCED_SCORE = 1e6
NEG_INF = -1e30
TINY = 1e-30
EPS = 1e-6
LANES = 128
N_GATE = 3 * NSA_HEADS
VMEM_LIMIT = 56 * 1024 * 1024


def _dot(a, b):
    return jnp.dot(a, b, preferred_element_type=F32)


def _dot_nt(a, b):
    return lax.dot_general(a, b, (((1,), (1,)), ((), ())), preferred_element_type=F32)


def _params(*sem):
    return pltpu.CompilerParams(dimension_semantics=sem, vmem_limit_bytes=VMEM_LIMIT)


def _resident(shape):
    nd = len(shape)
    return pl.BlockSpec(shape, lambda *_: (0,) * nd, pipeline_mode=pl.Buffered(1))


def _lane_iota(shape):
    return lax.broadcasted_iota(jnp.int32, shape, len(shape) - 1)


def _row_iota(shape):
    return lax.broadcasted_iota(jnp.int32, shape, len(shape) - 2)


def _gelu(x):
    return 0.5 * x * (1.0 + jnp.tanh(0.7978845608028654 * (x + 0.044715 * (x * x * x))))


def _sigmoid(x):
    return 1.0 / (1.0 + jnp.exp(-x))


def _log_sigmoid(x):
    return -(jnp.maximum(-x, 0.0) + jnp.log1p(jnp.exp(-jnp.abs(x))))


def _softmax_rows(logits, mask):
    lg = jnp.where(mask, logits, NEG_INF)
    m = jnp.max(lg, axis=-1, keepdims=True)
    e = jnp.where(mask, jnp.exp(lg - m), 0.0)
    return e / jnp.maximum(jnp.sum(e, axis=-1, keepdims=True), TINY)


_C_QA = 0
_C_KV = _C_QA + D_QA
_C_FQ = _C_KV + 6 * D_KVA
_C_FK = _C_FQ + D_FOX
_C_FV = _C_FK + D_FOX
_C_GA = _C_FV + D_FOX
_C_GB = _C_GA + D_MODEL
_C_MISC = _C_GB + D_MODEL
_C_END = _C_MISC + LANES


def _qa_perm():
    cols = []
    for p in range(NSA_GROUP):
        cols += list(range(p * HEAD_DIM, (p + 1) * HEAD_DIM))
        cols += list(range((NSA_GROUP + p) * HEAD_DIM, (NSA_GROUP + p + 1) * HEAD_DIM))
    return np.asarray(cols, np.int32)


def _permute_w_in(w_in):
    offs = np.concatenate([[0], np.cumsum(IN_SIZES)])
    o_ng, o_fq, o_ff, o_ga = int(offs[7]), int(offs[8]), int(offs[11]), int(offs[12])
    main = np.concatenate([_qa_perm(), np.arange(D_QA, o_ng), np.arange(o_fq, o_ff), np.arange(o_ga, offs[-1])])
    pad = jnp.zeros((w_in.shape[0], LANES - N_GATE - FOX_HEADS), w_in.dtype)
    w = jnp.concatenate([w_in[:, main], w_in[:, o_ng:o_fq], w_in[:, o_ff:o_ga], pad], axis=1)
    return w.astype(BF16)


def _head_mean_matrix():
    i = np.arange(D_QA)
    return jnp.asarray((i[:, None] // HEAD_DIM == i[None, :] // HEAD_DIM) / HEAD_DIM, BF16)


def _proj_kernel(x_ref, gattn_ref, w_ref, m64_ref, gq_ref, gsk_ref, gwk_ref, gfq_ref, gfk_ref, bf_ref,
                 qa_ref, ck_ref, cv_ref, sk_ref, sv_ref, wk_ref, wv_ref, skb_ref, svb_ref, wkb_ref, wvb_ref,
                 fq_ref, fk_ref, fv_ref, fkb_ref, fvb_ref, misc_ref, ga_ref, gb_ref):
    x = x_ref[...]
    ms = jnp.mean(x * x, axis=-1, keepdims=True)
    xn = (x * lax.rsqrt(ms + EPS) * gattn_ref[...]).astype(BF16)

    def seg(a, width):
        return _dot(xn, w_ref[:, a:a + width])

    def head_norm(z, g_ref):
        w = z.shape[-1]
        hm = _dot((z * z).astype(BF16), m64_ref[0:w, 0:w])
        return z * lax.rsqrt(hm + EPS) * g_ref[...]

    qa_ref[...] = head_norm(seg(_C_QA, D_QA), gq_ref).astype(BF16)
    ck_ref[...] = seg(_C_KV, D_KVA)
    cv_ref[...] = seg(_C_KV + D_KVA, D_KVA)
    sk = head_norm(seg(_C_KV + 2 * D_KVA, D_KVA), gsk_ref)
    sk_ref[...] = sk
    skb_ref[...] = sk.astype(BF16)
    sv = seg(_C_KV + 3 * D_KVA, D_KVA)
    sv_ref[...] = sv
    svb_ref[...] = sv.astype(BF16)
    wk = head_norm(seg(_C_KV + 4 * D_KVA, D_KVA), gwk_ref)
    wk_ref[...] = wk
    wkb_ref[...] = wk.astype(BF16)
    wv = seg(_C_KV + 5 * D_KVA, D_KVA)
    wv_ref[...] = wv
    wvb_ref[...] = wv.astype(BF16)
    fq_ref[...] = head_norm(seg(_C_FQ, D_FOX), gfq_ref).astype(BF16)
    fk = head_norm(seg(_C_FK, D_FOX), gfk_ref)
    fk_ref[...] = fk
    fkb_ref[...] = fk.astype(BF16)
    fv = seg(_C_FV, D_FOX)
    fv_ref[...] = fv
    fvb_ref[...] = fv.astype(BF16)
    ga_ref[...] = _sigmoid(seg(_C_GA, D_MODEL))
    gb_ref[...] = _sigmoid(seg(_C_GB, D_MODEL))
    zm = seg(_C_MISC, LANES)
    lane = _lane_iota(zm.shape)
    misc_ref[...] = jnp.where(lane < N_GATE, _sigmoid(zm),
                              jnp.where(lane < N_GATE + FOX_HEADS, _log_sigmoid(zm + bf_ref[...]), 0.0))


def _project(x2d, wts, tm):
    n = x2d.shape[0]
    assert n % tm == 0
    row = lambda w: pl.BlockSpec((tm, w), lambda i: (i, 0))
    outs = [(D_QA, BF16)] + [(D_KVA, F32)] * 6 + [(D_KVA, BF16)] * 4 + [(D_FOX, BF16), (D_FOX, F32), (D_FOX, F32),
            (D_FOX, BF16), (D_FOX, BF16), (LANES, F32), (D_MODEL, F32), (D_MODEL, F32)]
    small = [wts['g_attn'], wts['w_in'], wts['m64'], wts['gq'], wts['gsk'], wts['gwk'], wts['gfq'], wts['gfk'],
             wts['bf']]
    res = pl.pallas_call(
        _proj_kernel,
        grid=(n // tm,),
        in_specs=[row(D_MODEL)] + [_resident(a.shape) for a in small],
        out_specs=[row(w) for w, _ in outs],
        out_shape=[jax.ShapeDtypeStruct((n, w), dt) for w, dt in outs],
        compiler_params=_params("parallel"),
        name="proj",
    )(x2d, *small)
    names = ('qa', 'ck', 'cv', 'sk', 'sv', 'wk', 'wv', 'skb', 'svb', 'wkb', 'wvb', 'fq', 'fk', 'fv', 'fkb', 'fvb',
             'misc', 'ga', 'gb')
    return dict(zip(names, res))


CHUNK_LANES = CMP_STRIDE * D_KVA
CHUNKS_PER_PAGE = PAGE_SIZE // CMP_STRIDE


def _cmp_weights(pe, w1, b1, w2):
    eye = jnp.eye(NSA_KV_HEADS, dtype=w1.dtype)
    big = lambda w: jnp.einsum('pdh,gk->pgdkh', w, eye).reshape(CHUNK_LANES, NSA_KV_HEADS * CMP_HIDDEN).astype(BF16)
    flat = lambda e: jnp.broadcast_to(e[:, None, :], (CMP_STRIDE, NSA_KV_HEADS, HEAD_DIM)).reshape(1, CHUNK_LANES)
    w2bd = jnp.einsum('he,gk->ghke', w2, eye).reshape(NSA_KV_HEADS * CMP_HIDDEN, D_KVA).astype(BF16)
    return dict(w1lo=big(w1[:CMP_STRIDE]), w1hi=big(w1[CMP_STRIDE:]), pelo=flat(pe[:CMP_STRIDE]),
                pehi=flat(pe[CMP_STRIDE:]), b1=jnp.tile(b1, NSA_KV_HEADS)[None, :], w2=w2bd)


def _cmp_lh_body(x, pelo, pehi, w1lo, w1hi):
    lo = _dot((x + pelo).astype(BF16), w1lo)
    hi = _dot((x + pehi).astype(BF16), w1hi)
    return jnp.concatenate([lo, hi], axis=1)


def _cmp_lh_kernel(*refs, n_blocks):
    xk = refs[:n_blocks]
    xv = refs[n_blocks:2 * n_blocks]
    pk = refs[2 * n_blocks:2 * n_blocks + 4]
    pv = refs[2 * n_blocks + 4:2 * n_blocks + 8]
    ok_ref, ov_ref = refs[2 * n_blocks + 8:]

    def rows(blocks):
        parts = [b[...].reshape(b.shape[-2], b.shape[-1]) for b in blocks]
        return parts[0] if len(parts) == 1 else jnp.concatenate(parts, axis=0)

    ok_ref[...] = _cmp_lh_body(rows(xk), pk[0][...], pk[1][...], pk[2][...], pk[3][...])
    ov_ref[...] = _cmp_lh_body(rows(xv), pv[0][...], pv[1][...], pv[2][...], pv[3][...])


def _cmp_lh_dense(xk, xv, cwk, cwv, tm):
    n = xk.shape[0]
    assert n % tm == 0
    row = lambda w: pl.BlockSpec((tm, w), lambda i: (i, 0))
    par = [cwk['pelo'], cwk['pehi'], cwk['w1lo'], cwk['w1hi'], cwv['pelo'], cwv['pehi'], cwv['w1lo'], cwv['w1hi']]
    wlh = 2 * NSA_KV_HEADS * CMP_HIDDEN
    return pl.pallas_call(
        functools.partial(_cmp_lh_kernel, n_blocks=1),
        grid=(n // tm,),
        in_specs=[row(CHUNK_LANES), row(CHUNK_LANES)] + [_resident(a.shape) for a in par],
        out_specs=[row(wlh), row(wlh)],
        out_shape=[jax.ShapeDtypeStruct((n, wlh), F32)] * 2,
        compiler_params=_params("parallel"),
        name="cmp_lh",
    )(xk, xv, *par)


def _cmp_lh_paged(pool_k, pool_v, page_table, cwk, cwv, pages_per_step):
    nb, n_pages = page_table.shape
    assert n_pages % pages_per_step == 0
    steps = n_pages // pages_per_step

    def page_spec(u):
        return pl.BlockSpec((1, CHUNKS_PER_PAGE, CHUNK_LANES),
                            lambda b, j, pt: (pt[b, j * pages_per_step + u], 0, 0))

    par = [cwk['pelo'], cwk['pehi'], cwk['w1lo'], cwk['w1hi'], cwv['pelo'], cwv['pehi'], cwv['w1lo'], cwv['w1hi']]
    wlh = 2 * NSA_KV_HEADS * CMP_HIDDEN
    rows = pages_per_step * CHUNKS_PER_PAGE
    out_spec = pl.BlockSpec((rows, wlh), lambda b, j, pt: (b * steps + j, 0))
    const = lambda a: pl.BlockSpec(a.shape, lambda b, j, pt: (0,) * a.ndim, pipeline_mode=pl.Buffered(1))
    def kern(pt_ref, *refs):
        _cmp_lh_kernel(*refs, n_blocks=pages_per_step)

    return pl.pallas_call(
        kern,
        grid_spec=pltpu.PrefetchScalarGridSpec(
            num_scalar_prefetch=1,
            grid=(nb, steps),
            in_specs=[page_spec(u) for u in range(pages_per_step)] * 2 + [const(a) for a in par],
            out_specs=[out_spec, out_spec],
        ),
        out_shape=[jax.ShapeDtypeStruct((nb * n_pages * CHUNKS_PER_PAGE, wlh), F32)] * 2,
        compiler_params=_params("parallel", "parallel"),
        name="cmp_lh_paged",
    )(page_table, *([pool_k] * pages_per_step), *([pool_v] * pages_per_step), *par)


def _cmp_finish_kernel(lhk_ref, lhv_ref, b1k_ref, w2k_ref, b1v_ref, w2v_ref, m64_ref, gk_ref, kc_ref, vc_ref):
    def finish(lh_ref, b1_ref, w2_ref):
        lh = lh_ref[...]
        nc = lh.shape[0]
        half = lh.shape[1] // 2
        hi_next = pltpu.roll(lh[:, half:], nc - 1, 0)
        h = _gelu(lh[:, :half] + hi_next + b1_ref[...])
        return _dot(h.astype(BF16), w2_ref[...])

    kc = finish(lhk_ref, b1k_ref, w2k_ref)
    hm = _dot((kc * kc).astype(BF16), m64_ref[0:D_KVA, 0:D_KVA])
    kc_ref[...] = (kc * lax.rsqrt(hm + EPS) * gk_ref[...]).astype(BF16)
    vc_ref[...] = finish(lhv_ref, b1v_ref, w2v_ref).astype(BF16)


def _cmp_finish(lhk, lhv, cwk, cwv, m64, gk, nc):
    n = lhk.shape[0]
    nb = n // nc
    seq = lambda w: pl.BlockSpec((nc, w), lambda b: (b, 0))
    par = [cwk['b1'], cwk['w2'], cwv['b1'], cwv['w2'], m64, gk]
    return pl.pallas_call(
        _cmp_finish_kernel,
        grid=(nb,),
        in_specs=[seq(lhk.shape[1]), seq(lhv.shape[1])] + [_resident(a.shape) for a in par],
        out_specs=[seq(D_KVA), seq(D_KVA)],
        out_shape=[jax.ShapeDtypeStruct((n, D_KVA), BF16)] * 2,
        compiler_params=_params("parallel"),
        name="cmp_finish",
    )(lhk, lhv, *par)


def _alibi_slope(g, r):
    return 2.0 ** -(NSA_GROUP * g + r + 1)


def _stack_heads(q):
    return jnp.concatenate([q[:, p * LANES:(p + 1) * LANES] for p in range(NSA_GROUP)], axis=0)


def _group_lanes(x, g):
    return jnp.where((_lane_iota(x.shape) // HEAD_DIM) == g, x, jnp.zeros_like(x))


def _unstack_heads(o0, o1, tq):
    lo = _lane_iota((tq, LANES)) < HEAD_DIM
    return jnp.concatenate([jnp.where(lo, o0[p * tq:(p + 1) * tq], o1[p * tq:(p + 1) * tq])
                            for p in range(NSA_GROUP)], axis=1)


def _cmp_attn_kernel(q_ref, kc_ref, vc_ref, o_ref, sel_ref, idx_ref, *, tq, n_slc, pos0):
    nck = kc_ref.shape[0]
    nsl = sel_ref.shape[-1]
    qpos0 = pos0 + pl.program_id(1) * tq
    qs = _stack_heads(q_ref[...])
    kc = kc_ref[...]
    vc = vc_ref[...]
    dist = (qpos0 + _row_iota((tq, nck))) - (_lane_iota((tq, nck)) * CMP_STRIDE + (CMP_BLOCK - 1))
    mask = dist >= 0
    distf = dist.astype(F32)
    c_start = _row_iota((nck, nsl)) * CMP_STRIDE
    s_start = _lane_iota((nck, nsl)) * SEL_BLOCK
    overlap = jnp.where((c_start < s_start + SEL_BLOCK) & (c_start + CMP_BLOCK > s_start), 1.0, 0.0).astype(BF16)
    blk = _lane_iota((tq, nsl))
    blkf = blk.astype(F32)
    qp = qpos0 + _row_iota((tq, nsl))
    cur = lax.shift_right_logical(qp, 6)
    forced = (blk == 0) | (blk == cur) | (blk == cur - 1)
    avail = blk * SEL_BLOCK <= qp
    slot = _lane_iota((tq, LANES))
    outs = []
    for g in range(NSA_KV_HEADS):
        s = _dot_nt(qs, _group_lanes(kc, g))
        ps = [_softmax_rows(s[r * tq:(r + 1) * tq] - _alibi_slope(g, r) * distf, mask) for r in range(NSA_GROUP)]
        outs.append(_dot(jnp.concatenate(ps, axis=0).astype(BF16), vc))
        psum = (ps[0] + ps[1]) + (ps[2] + ps[3])
        hi = psum.astype(BF16)
        lo = (psum - hi.astype(F32)).astype(BF16)
        p_slc = _dot(hi, overlap) + _dot(lo, overlap)
        score = jnp.where(forced, FORCED_SCORE, jnp.where(avail, p_slc, -1.0))
        score = jnp.where(blk < n_slc, score, -2.0)
        sel = jnp.zeros((tq, nsl), F32)
        idx = jnp.zeros((tq, LANES), F32)
        for it in range(N_SEL):
            m = jnp.max(score, axis=-1, keepdims=True)
            j = jnp.min(jnp.where(score == m, blkf, 1e9), axis=-1, keepdims=True)
            hit = blkf == j
            sel = jnp.where(hit, 1.0, sel)
            score = jnp.where(hit, -3.0, score)
            idx = jnp.where(slot == it, j, idx)
        sel_ref[0, g] = sel.astype(BF16)
        idx_ref[0, g] = idx.astype(jnp.int32)
    o_ref[...] = _unstack_heads(outs[0], outs[1], tq)


def _cmp_attend(q2d, kc2d, vc2d, nb, t, tq, nck, n_slc, pos0):
    nsl = -(-n_slc // LANES) * LANES
    nt = t // tq
    return pl.pallas_call(
        functools.partial(_cmp_attn_kernel, tq=tq, n_slc=n_slc, pos0=pos0),
        grid=(nb, nt),
        in_specs=[pl.BlockSpec((tq, D_QA), lambda b, i: (b * nt + i, 0)),
                  pl.BlockSpec((nck, D_KVA), lambda b, i: (b, 0)),
                  pl.BlockSpec((nck, D_KVA), lambda b, i: (b, 0))],
        out_specs=[pl.BlockSpec((tq, D_QA), lambda b, i: (b * nt + i, 0)),
                   pl.BlockSpec((1, NSA_KV_HEADS, tq, nsl), lambda b, i: (b, 0, i, 0)),
                   pl.BlockSpec((1, NSA_KV_HEADS, tq, LANES), lambda b, i: (b, 0, i, 0))],
        out_shape=[jax.ShapeDtypeStruct((nb * t, D_QA), F32),
                   jax.ShapeDtypeStruct((nb, NSA_KV_HEADS, t, nsl), BF16),
                   jax.ShapeDtypeStruct((nb, NSA_KV_HEADS, t, LANES), jnp.int32)],
        compiler_params=_params("parallel", "parallel"),
        name="cmp_attn",
    )(q2d, kc2d, vc2d)


def _sel_prompt_kernel(q_ref, k_ref, v_ref, sel_ref, o_ref, m_ref, l_ref, acc_ref, *, tq, tk):
    i = pl.program_id(1)
    nsl = sel_ref.shape[-1]
    qpos0 = i * tq
    qs = _stack_heads(q_ref[...])
    n_kt = ((i + 1) * tq + tk - 1) // tk
    outs = []
    for g in range(NSA_KV_HEADS):
        m_ref[...] = jnp.full(m_ref.shape, NEG_INF, F32)
        l_ref[...] = jnp.zeros(l_ref.shape, F32)
        acc_ref[...] = jnp.zeros(acc_ref.shape, F32)
        sel_g = sel_ref[0, g]

        def body(kt, carry):
            k0 = pl.multiple_of(kt * tk, tk)
            kg = _group_lanes(k_ref[pl.ds(k0, tk), :], g)
            vt = v_ref[pl.ds(k0, tk), :]
            s = _dot_nt(qs, kg)
            kpos = k0 + _lane_iota((tq, tk))
            dist = (qpos0 + _row_iota((tq, tk))) - kpos
            expand = jnp.where(lax.shift_right_logical(k0 + _lane_iota((nsl, tk)), 6) == _row_iota((nsl, tk)),
                               1.0, 0.0).astype(BF16)
            mask = (_dot(sel_g, expand) > 0.5) & (dist >= 0)
            distf = dist.astype(F32)
            ps, alphas = [], []
            for r in range(NSA_GROUP):
                rows = slice(r * tq, (r + 1) * tq)
                lg = jnp.where(mask, s[rows] - _alibi_slope(g, r) * distf, NEG_INF)
                m_old = m_ref[rows, :]
                m_new = jnp.maximum(m_old, jnp.max(lg, axis=-1, keepdims=True))
                p = jnp.where(mask, jnp.exp(lg - m_new), 0.0)
                alpha = jnp.exp(m_old - m_new)
                l_ref[rows, :] = alpha * l_ref[rows, :] + jnp.sum(p, axis=-1, keepdims=True)
                m_ref[rows, :] = m_new
                ps.append(p.astype(BF16))
                alphas.append(alpha)
            pv = _dot(jnp.concatenate(ps, axis=0), vt)
            acc_ref[...] = jnp.concatenate(alphas, axis=0) * acc_ref[...] + pv
            return carry

        lax.fori_loop(0, n_kt, body, 0)
        outs.append(acc_ref[...] / jnp.maximum(l_ref[...], TINY))
    o_ref[...] = _unstack_heads(outs[0], outs[1], tq)


def _sel_prompt(q2d, k2d, v2d, sel, nb, t, tq, tk):
    nt = t // tq
    nsl = sel.shape[-1]
    return pl.pallas_call(
        functools.partial(_sel_prompt_kernel, tq=tq, tk=tk),
        grid=(nb, nt),
        in_specs=[pl.BlockSpec((tq, D_QA), lambda b, i: (b * nt + i, 0)),
                  pl.BlockSpec((t, D_KVA), lambda b, i: (b, 0)),
                  pl.BlockSpec((t, D_KVA), lambda b, i: (b, 0)),
                  pl.BlockSpec((1, NSA_KV_HEADS, tq, nsl), lambda b, i: (b, 0, i, 0))],
        out_specs=pl.BlockSpec((tq, D_QA), lambda b, i: (b * nt + i, 0)),
        out_shape=jax.ShapeDtypeStruct((nb * t, D_QA), F32),
        scratch_shapes=[pltpu.VMEM((NSA_GROUP * tq, 1), F32), pltpu.VMEM((NSA_GROUP * tq, 1), F32),
                        pltpu.VMEM((NSA_GROUP * tq, LANES), F32)],
        compiler_params=_params("parallel", "parallel"),
        name="sel_prompt",
    )(q2d, k2d, v2d, sel)


def _window_core(q, k, v, qpos0, kpos0, tq):
    nk = k.shape[0]
    qs = _stack_heads(q)
    kpos = kpos0 + _lane_iota((tq, nk))
    dist = (qpos0 + _row_iota((tq, nk))) - kpos
    mask = (dist >= 0) & (dist <= WINDOW) & (kpos >= 0)
    distf = dist.astype(F32)
    outs = []
    for g in range(NSA_KV_HEADS):
        s = _dot_nt(qs, _group_lanes(k, g))
        ps = [_softmax_rows(s[r * tq:(r + 1) * tq] - _alibi_slope(g, r) * distf, mask).astype(BF16)
              for r in range(NSA_GROUP)]
        outs.append(_dot(jnp.concatenate(ps, axis=0), v))
    return _unstack_heads(outs[0], outs[1], tq)


def _win_prompt_kernel(q_ref, k_ref, v_ref, o_ref, *, tq, span):
    i = pl.program_id(1)
    k0 = pl.multiple_of(jnp.maximum(i * tq + tq - span, 0), tq)
    o_ref[...] = _window_core(q_ref[...], k_ref[pl.ds(k0, span), :], v_ref[pl.ds(k0, span), :], i * tq, k0, tq)


def _win_prompt(q2d, k2d, v2d, nb, t, tq):
    nt = t // tq
    span = min(tq + WINDOW, t)
    return pl.pallas_call(
        functools.partial(_win_prompt_kernel, tq=tq, span=span),
        grid=(nb, nt),
        in_specs=[pl.BlockSpec((tq, D_QA), lambda b, i: (b * nt + i, 0)),
                  pl.BlockSpec((t, D_KVA), lambda b, i: (b, 0)),
                  pl.BlockSpec((t, D_KVA), lambda b, i: (b, 0))],
        out_specs=pl.BlockSpec((tq, D_QA), lambda b, i: (b * nt + i, 0)),
        out_shape=jax.ShapeDtypeStruct((nb * t, D_QA), F32),
        compiler_params=_params("parallel", "parallel"),
        name="win_prompt",
    )(q2d, k2d, v2d)


def _split3(x):
    a = x.astype(BF16)
    r = x - a.astype(F32)
    b = r.astype(BF16)
    c = (r - b.astype(F32)).astype(BF16)
    return a, b, c


def _dot3(x, m):
    a, b, c = _split3(x)
    return _dot(a, m) + (_dot(b, m) + _dot(c, m))


def _fox_bias_kernel(misc_ref, o_ref, *, t):
    tri = jnp.where(_lane_iota((LANES, LANES)) <= _row_iota((LANES, LANES)), 1.0, 0.0).astype(BF16)
    carry = jnp.zeros((1, LANES), F32)
    for j in range(t // LANES):
        lf = misc_ref[j * LANES:(j + 1) * LANES, :]
        c = _dot3_left(tri, lf) + carry
        carry = c[LANES - 1:LANES, :]
        o_ref[0, :, j * LANES:(j + 1) * LANES] = -(c.T[N_GATE:N_GATE + FOX_HEADS, :])


def _dot3_left(m, x):
    a, b, c = _split3(x)
    return _dot(m, a) + (_dot(m, b) + _dot(m, c))


def _fox_bias(misc2d, nb, t):
    return pl.pallas_call(
        functools.partial(_fox_bias_kernel, t=t),
        grid=(nb,),
        in_specs=[pl.BlockSpec((t, LANES), lambda b: (b, 0))],
        out_specs=pl.BlockSpec((1, FOX_HEADS, t), lambda b: (b, 0, 0)),
        out_shape=jax.ShapeDtypeStruct((nb, FOX_HEADS, t), F32),
        compiler_params=_params("parallel"),
        name="fox_bias",
    )(misc2d)


def _fox_prompt_kernel(q_ref, k_ref, v_ref, b_ref, o_ref, m_ref, l_ref, acc_ref, *, tq, tk):
    i = pl.program_id(1)
    qpos0 = i * tq
    n_kt = ((i + 1) * tq + tk - 1) // tk
    lo_lanes = _lane_iota((tq, LANES)) < HEAD_DIM
    for p in range(FOX_HEADS // 2):
        qp = q_ref[:, p * LANES:(p + 1) * LANES]
        q2 = jnp.concatenate([jnp.where(lo_lanes, qp, jnp.zeros_like(qp)),
                              jnp.where(lo_lanes, jnp.zeros_like(qp), qp)], axis=0)
        m_ref[...] = jnp.full(m_ref.shape, NEG_INF, F32)
        l_ref[...] = jnp.zeros(l_ref.shape, F32)
        acc_ref[...] = jnp.zeros(acc_ref.shape, F32)

        def body(kt, carry):
            k0 = pl.multiple_of(kt * tk, tk)
            s = _dot_nt(q2, k_ref[pl.ds(k0, tk), p * LANES:(p + 1) * LANES])
            mask = (k0 + _lane_iota((tq, tk))) <= (qpos0 + _row_iota((tq, tk)))
            ps, alphas = [], []
            for hh in range(2):
                rows = slice(hh * tq, (hh + 1) * tq)
                bias = b_ref[0, 2 * p + hh:2 * p + hh + 1, pl.ds(k0, tk)]
                lg = jnp.where(mask, s[rows] + bias, NEG_INF)
                m_old = m_ref[rows, :]
                m_new = jnp.maximum(m_old, jnp.max(lg, axis=-1, keepdims=True))
                pr = jnp.where(mask, jnp.exp(lg - m_new), 0.0)
                alpha = jnp.exp(m_old - m_new)
                l_ref[rows, :] = alpha * l_ref[rows, :] + jnp.sum(pr, axis=-1, keepdims=True)
                m_ref[rows, :] = m_new
                ps.append(pr.astype(BF16))
                alphas.append(alpha)
            pv = _dot(jnp.concatenate(ps, axis=0), v_ref[pl.ds(k0, tk), p * LANES:(p + 1) * LANES])
            acc_ref[...] = jnp.concatenate(alphas, axis=0) * acc_ref[...] + pv
            return carry

        lax.fori_loop(0, n_kt, body, 0)
        o = acc_ref[...] / jnp.maximum(l_ref[...], TINY)
        o_ref[:, p * LANES:(p + 1) * LANES] = jnp.where(lo_lanes, o[:tq], o[tq:])


def _fox_prompt(q2d, k2d, v2d, bias, nb, t, tq, tk):
    nt = t // tq
    return pl.pallas_call(
        functools.partial(_fox_prompt_kernel, tq=tq, tk=tk),
        grid=(nb, nt),
        in_specs=[pl.BlockSpec((tq, D_FOX), lambda b, i: (b * nt + i, 0)),
                  pl.BlockSpec((t, D_FOX), lambda b, i: (b, 0)),
                  pl.BlockSpec((t, D_FOX), lambda b, i: (b, 0)),
                  pl.BlockSpec((1, FOX_HEADS, t), lambda b, i: (b, 0, 0))],
        out_specs=pl.BlockSpec((tq, D_FOX), lambda b, i: (b * nt + i, 0)),
        out_shape=jax.ShapeDtypeStruct((nb * t, D_FOX), F32),
        scratch_shapes=[pltpu.VMEM((2 * tq, 1), F32), pltpu.VMEM((2 * tq, 1), F32),
                        pltpu.VMEM((2 * tq, LANES), F32)],
        compiler_params=_params("parallel", "parallel"),
        name="fox_prompt",
    )(q2d, k2d, v2d, bias)


def _merge_kernel(x_ref, oc_ref, os_ref, ow_ref, misc_ref, of_ref, ga_ref, gb_ref, woa_ref, wob_ref, wout_ref, h_ref):
    tm = x_ref.shape[0]
    misc = misc_ref[...]
    lo_lanes = _lane_iota((tm, LANES)) < HEAD_DIM

    def gate(head, j):
        c = 3 * head + j
        return jnp.broadcast_to(misc[:, c:c + 1], (tm, LANES))

    chunks = []
    for p in range(NSA_GROUP):
        cols = slice(p * LANES, (p + 1) * LANES)
        acc = None
        for j, o_ref in enumerate((oc_ref, os_ref, ow_ref)):
            term = jnp.where(lo_lanes, gate(p, j), gate(NSA_GROUP + p, j)) * o_ref[:, cols]
            acc = term if acc is None else acc + term
        chunks.append(acc.astype(BF16))
    ya = _dot(jnp.concatenate(chunks, axis=1), woa_ref[...])
    yb = _dot(of_ref[...].astype(BF16), wob_ref[...])
    mix = (ga_ref[...] * ya + gb_ref[...] * yb).astype(BF16)
    h_ref[...] = x_ref[...] + _dot(mix, wout_ref[...])


def _merge(x2d, o_cmp, o_sel, o_win, misc, o_fox, ga, gb, wts, tm):
    n = x2d.shape[0]
    assert n % tm == 0
    row = lambda w: pl.BlockSpec((tm, w), lambda i: (i, 0))
    par = [wts['w_o_a'], wts['w_o_b'], wts['w_out']]
    return pl.pallas_call(
        _merge_kernel,
        grid=(n // tm,),
        in_specs=[row(D_MODEL), row(D_QA), row(D_QA), row(D_QA), row(LANES), row(D_FOX), row(D_MODEL), row(D_MODEL)]
        + [_resident(a.shape) for a in par],
        out_specs=row(D_MODEL),
        out_shape=jax.ShapeDtypeStruct((n, D_MODEL), F32),
        compiler_params=_params("parallel"),
        name="merge",
    )(x2d, o_cmp, o_sel, o_win, misc, o_fox, ga, gb, *par)


FFN_CHUNK = 512


def _ffn_kernel(h_ref, p1_ref, p2_ref, g_ref, wup_ref, wgate_ref, wdown_ref, cw_ref, cb_ref, y_ref, tail_ref,
                carry_ref, *, tm, t_seq, carry):
    i = pl.program_id(0)
    h = h_ref[...]
    ms = jnp.mean(h * h, axis=-1, keepdims=True)
    hn = (h * lax.rsqrt(ms + EPS) * g_ref[...]).astype(BF16)
    t_in_seq = (i * tm + _row_iota((tm, FFN_CHUNK))) % t_seq
    acc = jnp.zeros((tm, D_MODEL), F32)
    for f in range(D_FF // FFN_CHUNK):
        cols = slice(f * FFN_CHUNK, (f + 1) * FFN_CHUNK)
        u = _dot(hn, wup_ref[:, cols])
        gt = _dot(hn, wgate_ref[:, cols])
        r1 = pltpu.roll(u, 1, 0)
        r2 = pltpu.roll(u, 2, 0)
        if carry:
            prev = carry_ref[:, cols]
            first = (i * tm) % t_seq == 0
            prev = jnp.where(first, jnp.zeros_like(prev), prev)
            row = _row_iota((tm, FFN_CHUNK))
            r1 = jnp.where(row == 0, prev[7:8, :], r1)
            r2 = jnp.where(row == 0, prev[6:7, :], jnp.where(row == 1, prev[7:8, :], r2))
            carry_ref[:, cols] = u[tm - 8:tm, :]
        else:
            r1 = jnp.where(t_in_seq >= 1, r1, p1_ref[:, cols])
            r2 = jnp.where(t_in_seq >= 2, r2, p2_ref[:, cols])
        uc = cb_ref[:, cols] + cw_ref[0:1, cols] * r2 + cw_ref[1:2, cols] * r1 + cw_ref[2:3, cols] * u
        acc = acc + _dot((_gelu(uc) * gt).astype(BF16), wdown_ref[cols, :])
        tail_ref[:, cols] = u[tm - tail_ref.shape[0]:tm, :]
    y_ref[...] = h + acc


def _ffn(h2d, p1, p2, wts, tm, t_seq, tail_rows):
    n = h2d.shape[0]
    assert n % tm == 0 and tail_rows % 8 == 0
    carry = p1 is None
    assert (t_seq % tm == 0) if carry else (tm % t_seq == 0)
    row = lambda w: pl.BlockSpec((tm, w), lambda i: (i, 0))
    par = [wts['g_ffn'], wts['w_up'], wts['w_gate'], wts['w_down'], wts['conv_w'], wts['conv_b']]
    if carry:
        kern = lambda h_ref, *rest, **kw: _ffn_kernel(h_ref, None, None, *rest, **kw)
        acts, act_specs = [h2d], [row(D_MODEL)]
    else:
        kern = _ffn_kernel
        acts, act_specs = [h2d, p1, p2], [row(D_MODEL), row(D_FF), row(D_FF)]
    return pl.pallas_call(
        functools.partial(kern, tm=tm, t_seq=t_seq, carry=carry),
        grid=(n // tm,),
        in_specs=act_specs + [_resident(a.shape) for a in par],
        out_specs=[row(D_MODEL), pl.BlockSpec((tail_rows, D_FF), lambda i: (i, 0))],
        out_shape=[jax.ShapeDtypeStruct((n, D_MODEL), F32),
                   jax.ShapeDtypeStruct((n // tm * tail_rows, D_FF), F32)],
        scratch_shapes=[pltpu.VMEM((8, D_FF), F32)],
        compiler_params=_params("arbitrary"),
        name="ffn",
    )(*acts, *par)


def _prep_weights(p):
    tile = lambda g, n: jnp.tile(g.astype(F32), n)[None, :]
    bf = jnp.zeros((1, LANES), F32).at[0, N_GATE:N_GATE + FOX_HEADS].set(p['b_forget'].astype(F32))
    return dict(
        g_attn=p['norm_attn_g'].astype(F32)[None, :],
        w_in=_permute_w_in(p['w_in']),
        m64=_head_mean_matrix(),
        gq=tile(p['q_norm_a_g'], NSA_HEADS) * ATTN_SCALE,
        gsk=tile(p['k_norm_sel_g'], NSA_KV_HEADS),
        gwk=tile(p['k_norm_win_g'], NSA_KV_HEADS),
        gkc=tile(p['k_norm_cmp_g'], NSA_KV_HEADS),
        gfq=tile(p['q_norm_b_g'], FOX_HEADS) * ATTN_SCALE,
        gfk=tile(p['k_norm_b_g'], FOX_HEADS),
        bf=bf,
        cmpk=_cmp_weights(p['cmp_pe_k'], p['cmp_w1_k'], p['cmp_b1_k'], p['cmp_w2_k']),
        cmpv=_cmp_weights(p['cmp_pe_v'], p['cmp_w1_v'], p['cmp_b1_v'], p['cmp_w2_v']),
        w_o_a=p['w_o_a'][_qa_perm()].astype(BF16),
        w_o_b=p['w_o_b'].astype(BF16),
        w_out=p['w_out'].astype(BF16),
        g_ffn=p['norm_ffn_g'].astype(F32)[None, :],
        w_up=p['w_up'].astype(BF16),
        w_gate=p['w_gate'].astype(BF16),
        w_down=p['w_down'].astype(BF16),
        conv_w=p['conv_w'].astype(F32),
        conv_b=p['conv_b'].astype(F32)[None, :],
    )


Q_PAD = 16


def _sel_sample_kernel(pt_ref, ix_ref, q_ref, *refs, n_cached, past):
    kb, vb = refs[:N_SEL], refs[N_SEL:2 * N_SEL]
    knew_ref, vnew_ref, o_ref = refs[2 * N_SEL:]
    b, t, g = pl.program_id(0), pl.program_id(1), pl.program_id(2)
    nk = (N_SEL + 1) * SEL_BLOCK
    kcat = jnp.concatenate([r[0] for r in kb] + [knew_ref[0]], axis=0).astype(BF16)
    vcat = jnp.concatenate([r[0] for r in vb] + [vnew_ref[0]], axis=0).astype(BF16)
    lane = _lane_iota((1, nk))
    slot = lane // SEL_BLOCK
    kpos = (lane % SEL_BLOCK) + past
    valid = slot == N_SEL
    for j in range(N_SEL):
        blk = ix_ref[b, g * (ix_ref.shape[1] // NSA_KV_HEADS) + t * N_SEL + j]
        here = slot == j
        kpos = jnp.where(here, (lane % SEL_BLOCK) + blk * SEL_BLOCK, kpos)
        valid = valid | (here & (blk < n_cached))
    qpos = past + t
    dist = qpos - kpos
    mask = valid & (dist >= 0)
    row = _row_iota((Q_PAD, 1))
    slope = jnp.where(row == 0, 0.5, jnp.where(row == 1, 0.25, jnp.where(row == 2, 0.125, 0.0625)))
    slope = slope * jnp.where(g == 0, 1.0, 2.0 ** -NSA_GROUP)
    s = _dot_nt(q_ref[0, 0], _group_lanes(kcat, g))
    p = _softmax_rows(s - slope * dist.astype(F32), jnp.broadcast_to(mask, s.shape))
    o_ref[0, 0, 0] = _dot(p.astype(BF16), vcat)[0:8]


def _sel_sample(qsel, pool_k, pool_v, knew, vnew, page_table, ix, past):
    nb, nt = qsel.shape[:2]
    n_cached = past // SEL_BLOCK
    halves = PAGE_SIZE // SEL_BLOCK

    def blk_spec(j):
        def imap(b, t, g, pt, ixr):
            blk = jnp.minimum(ixr[b, g * (nt * N_SEL) + t * N_SEL + j], n_cached - 1)
            return (pt[b, blk // halves] * halves + blk % halves, 0, 0)
        return pl.BlockSpec((1, SEL_BLOCK, D_KVA), imap)

    new_spec = pl.BlockSpec((1, SEL_BLOCK, D_KVA), lambda b, t, g, pt, ixr: (b, 0, 0))
    return pl.pallas_call(
        functools.partial(_sel_sample_kernel, n_cached=n_cached, past=past),
        grid_spec=pltpu.PrefetchScalarGridSpec(
            num_scalar_prefetch=2,
            grid=(nb, nt, NSA_KV_HEADS),
            in_specs=[pl.BlockSpec((1, 1, Q_PAD, LANES), lambda b, t, g, pt, ixr: (b, t, 0, 0))]
            + [blk_spec(j) for j in range(N_SEL)] * 2 + [new_spec, new_spec],
            out_specs=pl.BlockSpec((1, 1, 1, 8, LANES), lambda b, t, g, pt, ixr: (b, t, g, 0, 0)),
        ),
        out_shape=jax.ShapeDtypeStruct((nb, nt, NSA_KV_HEADS, 8, LANES), F32),
        compiler_params=_params("parallel", "parallel", "parallel"),
        name="sel_sample",
    )(page_table, ix, qsel, *([pool_k] * N_SEL), *([pool_v] * N_SEL), knew, vnew)


def _win_sample_kernel(q_ref, k_ref, v_ref, o_ref, *, past, w_buf):
    o_ref[...] = _window_core(q_ref[...], k_ref[0].astype(BF16), v_ref[0].astype(BF16), past, past - w_buf, Q_PAD)


def _win_sample(qpad, k_all, v_all, past, w_buf):
    nb, nk = k_all.shape[:2]
    return pl.pallas_call(
        functools.partial(_win_sample_kernel, past=past, w_buf=w_buf),
        grid=(nb,),
        in_specs=[pl.BlockSpec((Q_PAD, D_QA), lambda b: (b, 0)),
                  pl.BlockSpec((1, nk, D_KVA), lambda b: (b, 0, 0)),
                  pl.BlockSpec((1, nk, D_KVA), lambda b: (b, 0, 0))],
        out_specs=pl.BlockSpec((Q_PAD, D_QA), lambda b: (b, 0)),
        out_shape=jax.ShapeDtypeStruct((nb * Q_PAD, D_QA), F32),
        compiler_params=_params("parallel"),
        name="win_sample",
    )(qpad, k_all, v_all)


def _fox_decode_kernel(pt_ref, q_ref, *refs, pps, n_new):
    kp, vp, lp = refs[:pps], refs[pps:2 * pps], refs[2 * pps:3 * pps]
    knew_ref, vnew_ref, lnew_ref, o_ref, m_ref, l_ref, acc_ref, carry_ref = refs[3 * pps:]
    j = pl.program_id(1)
    rows = n_new * FOX_HEADS

    @pl.when(j == 0)
    def _():
        m_ref[...] = jnp.full(m_ref.shape, NEG_INF, F32)
        l_ref[...] = jnp.zeros(l_ref.shape, F32)
        acc_ref[...] = jnp.zeros(acc_ref.shape, F32)
        carry_ref[...] = jnp.zeros(carry_ref.shape, F32)

    q = q_ref[...].astype(F32)
    head_of_lane = _lane_iota((FOX_HEADS, D_FOX)) // HEAD_DIM
    own = head_of_lane == _row_iota((FOX_HEADS, D_FOX))
    wq = jnp.concatenate([jnp.where(own, jnp.broadcast_to(q[t:t + 1, :], (FOX_HEADS, D_FOX)), 0.0)
                          for t in range(n_new)], axis=0).astype(BF16)

    def update(s, mask, v):
        lg = s if mask is None else jnp.where(mask, s, NEG_INF)
        m_old = m_ref[...]
        m_new = jnp.maximum(m_old, jnp.max(lg, axis=-1, keepdims=True))
        p = jnp.exp(lg - m_new)
        if mask is not None:
            p = jnp.where(mask, p, 0.0)
        alpha = jnp.exp(m_old - m_new)
        l_ref[...] = alpha * l_ref[...] + jnp.sum(p, axis=-1, keepdims=True)
        m_ref[...] = m_new
        acc_ref[...] = alpha * acc_ref[...] + _dot(p.astype(BF16), v)

    after = jnp.where(_row_iota((PAGE_SIZE, PAGE_SIZE)) > _lane_iota((PAGE_SIZE, PAGE_SIZE)), 1.0, 0.0).astype(BF16)
    carry = carry_ref[...]
    s_parts = []
    for u in range(pps):
        lft = lp[u][0]
        suffix = _dot3(lft, after) + carry[:, 0:1]
        carry = carry + jnp.sum(lft, axis=-1, keepdims=True)
        bias = jnp.concatenate([suffix] * n_new, axis=0)
        s_parts.append(_dot_nt(wq, kp[u][0].astype(BF16)) + bias)
    carry_ref[...] = carry
    vcat = jnp.concatenate([r[0].astype(BF16) for r in vp], axis=0)
    update(jnp.concatenate(s_parts, axis=1), None, vcat)

    @pl.when(j == pl.num_programs(1) - 1)
    def _():
        upto = jnp.where(_row_iota((PAGE_SIZE, PAGE_SIZE)) <= _lane_iota((PAGE_SIZE, PAGE_SIZE)), 1.0, 0.0)
        c_new = _dot3(lnew_ref[0], upto.astype(BF16))
        s_new = _dot_nt(wq, knew_ref[0].astype(BF16)) - jnp.concatenate([c_new] * n_new, axis=0)
        mask = _lane_iota((rows, PAGE_SIZE)) <= (_row_iota((rows, PAGE_SIZE)) // FOX_HEADS)
        update(s_new, mask, vnew_ref[0].astype(BF16))
        o = acc_ref[...] / jnp.maximum(l_ref[...], TINY)
        outs = [jnp.sum(jnp.where(own, o[t * FOX_HEADS:(t + 1) * FOX_HEADS], 0.0), axis=0, keepdims=True)
                for t in range(n_new)]
        o_ref[...] = jnp.concatenate(outs + [jnp.zeros((Q_PAD - n_new, D_FOX), F32)], axis=0)


def _fox_decode(qpad, pool_k, pool_v, pool_lft, knew, vnew, lnew_t, page_table, n_new, pps):
    nb, n_pages = page_table.shape
    assert n_pages % pps == 0
    steps = n_pages // pps

    def page_spec(u, shape):
        return pl.BlockSpec((1,) + shape, lambda b, j, pt: (pt[b, n_pages - 1 - (j * pps + u)], 0, 0))

    per_b = lambda shape: pl.BlockSpec((1,) + shape, lambda b, j, pt: (b, 0, 0))
    rows = n_new * FOX_HEADS
    return pl.pallas_call(
        functools.partial(_fox_decode_kernel, pps=pps, n_new=n_new),
        grid_spec=pltpu.PrefetchScalarGridSpec(
            num_scalar_prefetch=1,
            grid=(nb, steps),
            in_specs=[pl.BlockSpec((Q_PAD, D_FOX), lambda b, j, pt: (b, 0))]
            + [page_spec(u, (PAGE_SIZE, D_FOX)) for u in range(pps)] * 2
            + [page_spec(u, (FOX_HEADS, PAGE_SIZE)) for u in range(pps)]
            + [per_b((PAGE_SIZE, D_FOX)), per_b((PAGE_SIZE, D_FOX)), per_b((FOX_HEADS, PAGE_SIZE))],
            out_specs=pl.BlockSpec((Q_PAD, D_FOX), lambda b, j, pt: (b, 0)),
            scratch_shapes=[pltpu.VMEM((rows, 1), F32), pltpu.VMEM((rows, 1), F32), pltpu.VMEM((rows, D_FOX), F32),
                            pltpu.VMEM((FOX_HEADS, 1), F32)],
        ),
        out_shape=jax.ShapeDtypeStruct((nb * Q_PAD, D_FOX), F32),
        compiler_params=_params("parallel", "arbitrary"),
        name="fox_decode",
    )(page_table, qpad, *([pool_k] * pps), *([pool_v] * pps), *([pool_lft] * pps), knew, vnew, lnew_t)


def _pad_rows(a, rows):
    return jnp.pad(a, ((0, 0), (0, rows - a.shape[1]), (0, 0)))


def _sample_layer(x, cache, page_table, wts):
    (c_cmp_k, c_cmp_v, c_sel_k, c_sel_v, c_fox_k, c_fox_v, c_fox_logf, s_win_k, s_win_v, s_conv) = cache
    nb, t, _ = x.shape
    n = nb * t
    n_pool = c_cmp_k.shape[0]
    n_pages = page_table.shape[1]
    past = n_pages * PAGE_SIZE
    x2d = x.reshape(n, D_MODEL)
    pr = _project(x2d, wts, n)
    per_seq = lambda a: a.reshape(nb, t, a.shape[-1])
    qpad = _pad_rows(per_seq(pr['qa']), Q_PAD).reshape(nb * Q_PAD, D_QA)
    take = lambda o: o.reshape(nb, Q_PAD, -1)[:, :t].reshape(n, -1)

    nc = (past + t) // CMP_STRIDE
    assert nc * CMP_STRIDE <= past
    lhk, lhv = _cmp_lh_paged(c_cmp_k.reshape(n_pool, CHUNKS_PER_PAGE, CHUNK_LANES),
                             c_cmp_v.reshape(n_pool, CHUNKS_PER_PAGE, CHUNK_LANES),
                             page_table, wts['cmpk'], wts['cmpv'], min(8, n_pages))
    kc, vc = _cmp_finish(lhk, lhv, wts['cmpk'], wts['cmpv'], wts['m64'], wts['gkc'], nc)
    n_slc = -(-(past + t) // SEL_BLOCK)
    o_cmp, _, idx = _cmp_attend(qpad, kc, vc, nb, Q_PAD, Q_PAD, nc, n_slc, past)

    ix = idx[:, :, :t, :N_SEL].reshape(nb, NSA_KV_HEADS * t * N_SEL)
    qsel = _pad_rows(per_seq(pr['qa']).reshape(nb * t, NSA_GROUP, LANES), Q_PAD).reshape(nb, t, Q_PAD, LANES)
    halves = PAGE_SIZE // SEL_BLOCK
    o_sel = _sel_sample(qsel, c_sel_k.reshape(n_pool * halves, SEL_BLOCK, D_KVA),
                        c_sel_v.reshape(n_pool * halves, SEL_BLOCK, D_KVA),
                        _pad_rows(per_seq(pr['sk']), SEL_BLOCK), _pad_rows(per_seq(pr['sv']), SEL_BLOCK),
                        page_table, ix, past)
    lo = (jnp.arange(LANES) < HEAD_DIM)[None, None, None, :]
    o_sel = jnp.where(lo, o_sel[:, :, 0, :NSA_GROUP], o_sel[:, :, 1, :NSA_GROUP]).reshape(n, D_QA)

    w_buf = s_win_k.shape[1]
    wk_all = jnp.concatenate([s_win_k.reshape(nb, w_buf, D_KVA), per_seq(pr['wk'])], axis=1)
    wv_all = jnp.concatenate([s_win_v.reshape(nb, w_buf, D_KVA), per_seq(pr['wv'])], axis=1)
    o_win = _win_sample(qpad, _pad_rows(wk_all, w_buf + Q_PAD), _pad_rows(wv_all, w_buf + Q_PAD), past, w_buf)

    fqpad = _pad_rows(per_seq(pr['fq']), Q_PAD).reshape(nb * Q_PAD, D_FOX)
    logf = pr['misc'][:, N_GATE:N_GATE + FOX_HEADS]
    lnew_t = jnp.swapaxes(_pad_rows(per_seq(logf), PAGE_SIZE), 1, 2)
    o_fox = _fox_decode(fqpad, c_fox_k.reshape(n_pool, PAGE_SIZE, D_FOX), c_fox_v.reshape(n_pool, PAGE_SIZE, D_FOX),
                        jnp.swapaxes(c_fox_logf, 1, 2), _pad_rows(per_seq(pr['fk']), PAGE_SIZE),
                        _pad_rows(per_seq(pr['fv']), PAGE_SIZE), lnew_t, page_table, t, min(8, n_pages))

    h = _merge(x2d, take(o_cmp), o_sel, take(o_win), pr['misc'], take(o_fox), pr['ga'], pr['gb'], wts, n)
    zeros = jnp.zeros((nb, D_FF), F32)
    prev = s_conv.astype(F32)
    p1 = jnp.stack([prev[:, 1]] + [zeros] * (t - 1), axis=1).reshape(n, D_FF)
    p2 = jnp.stack([prev[:, 0], prev[:, 1]] + [zeros] * (t - 2), axis=1).reshape(n, D_FF)
    y, u = _ffn(h, p1, p2, wts, n, t, n)
    conv_rows = u.reshape(nb, t, D_FF)[:, t - (CONV_W - 1):]
    kv = lambda a: a.reshape(nb, -1, NSA_KV_HEADS, HEAD_DIM)
    fx = lambda a: a.reshape(nb, t, FOX_HEADS, HEAD_DIM)
    state = (kv(pr['ck']), kv(pr['cv']), kv(pr['sk']), kv(pr['sv']), fx(pr['fk']), fx(pr['fv']),
             logf.reshape(nb, t, FOX_HEADS), kv(wk_all[:, -w_buf:]), kv(wv_all[:, -w_buf:]), conv_rows)
    return y.reshape(nb, t, D_MODEL), state


def _tile_rows(n, pref):
    return pref if n % pref == 0 else n


def _prompt_layer(x, wts):
    nb, t, _ = x.shape
    n = nb * t
    x2d = x.reshape(n, D_MODEL)
    pr = _project(x2d, wts, _tile_rows(n, 256))
    nc = t // CMP_STRIDE
    lhk, lhv = _cmp_lh_dense(pr['ck'].reshape(n // CMP_STRIDE, CHUNK_LANES), pr['cv'].reshape(n // CMP_STRIDE, CHUNK_LANES),
                             wts['cmpk'], wts['cmpv'], _tile_rows(n // CMP_STRIDE, 256))
    kc, vc = _cmp_finish(lhk, lhv, wts['cmpk'], wts['cmpv'], wts['m64'], wts['gkc'], nc)
    n_slc = -(-t // SEL_BLOCK)
    o_cmp, sel, _ = _cmp_attend(pr['qa'], kc, vc, nb, t, 128, nc, n_slc, 0)
    o_sel = _sel_prompt(pr['qa'], pr['skb'], pr['svb'], sel, nb, t, 128, min(512, t))
    o_win = _win_prompt(pr['qa'], pr['wkb'], pr['wvb'], nb, t, 128)
    bias = _fox_bias(pr['misc'], nb, t)
    o_fox = _fox_prompt(pr['fq'], pr['fkb'], pr['fvb'], bias, nb, t, 256, min(512, t))
    h = _merge(x2d, o_cmp, o_sel, o_win, pr['misc'], o_fox, pr['ga'], pr['gb'], wts, _tile_rows(n, 256))
    tm = _tile_rows(t, 256)
    y, tail = _ffn(h, None, None, wts, tm, t, 8)
    conv_rows = tail.reshape(nb, t // tm, 8, D_FF)[:, -1, 8 - (CONV_W - 1):]
    kv = lambda a: a.reshape(nb, t, NSA_KV_HEADS, HEAD_DIM)
    fx = lambda a: a.reshape(nb, t, FOX_HEADS, HEAD_DIM)
    w_keep = min(WINDOW, t)
    logf = pr['misc'][:, N_GATE:N_GATE + FOX_HEADS].reshape(nb, t, FOX_HEADS)
    state = (kv(pr['ck']), kv(pr['cv']), kv(pr['sk']), kv(pr['sv']), fx(pr['fk']), fx(pr['fv']), logf,
             kv(pr['wk'])[:, t - w_keep:], kv(pr['wv'])[:, t - w_keep:], conv_rows)
    return y.reshape(nb, t, D_MODEL), state


_WEIGHT_NAMES = ('norm_attn_g', 'w_in', 'b_forget', 'q_norm_a_g', 'k_norm_cmp_g', 'k_norm_sel_g', 'k_norm_win_g',
                 'cmp_pe_k', 'cmp_w1_k', 'cmp_b1_k', 'cmp_w2_k', 'cmp_pe_v', 'cmp_w1_v', 'cmp_b1_v', 'cmp_w2_v',
                 'q_norm_b_g', 'k_norm_b_g', 'w_o_a', 'w_o_b', 'w_out', 'norm_ffn_g', 'w_up', 'w_gate', 'conv_w',
                 'conv_b', 'w_down')


def kernel(x_prompt, x_sample, cache_cmp_k, cache_cmp_v, cache_sel_k, cache_sel_v, cache_fox_k, cache_fox_v,
           cache_fox_logf, state_win_k, state_win_v, state_ffn_conv, page_table,
           norm_attn_g, w_in, b_forget, q_norm_a_g, k_norm_cmp_g, k_norm_sel_g, k_norm_win_g,
           cmp_pe_k, cmp_w1_k, cmp_b1_k, cmp_w2_k, cmp_pe_v, cmp_w1_v, cmp_b1_v, cmp_w2_v,
           q_norm_b_g, k_norm_b_g, w_o_a, w_o_b, w_out, norm_ffn_g, w_up, w_gate, conv_w, conv_b, w_down):
    weights = (norm_attn_g, w_in, b_forget, q_norm_a_g, k_norm_cmp_g, k_norm_sel_g, k_norm_win_g,
               cmp_pe_k, cmp_w1_k, cmp_b1_k, cmp_w2_k, cmp_pe_v, cmp_w1_v, cmp_b1_v, cmp_w2_v,
               q_norm_b_g, k_norm_b_g, w_o_a, w_o_b, w_out, norm_ffn_g, w_up, w_gate, conv_w, conv_b, w_down)
    caches = (cache_cmp_k, cache_cmp_v, cache_sel_k, cache_sel_v, cache_fox_k, cache_fox_v, cache_fox_logf,
              state_win_k, state_win_v, state_ffn_conv)
    depth = w_in.shape[0]
    prompt_states, sample_states = [], []
    for layer in range(depth):
        wts = _prep_weights({name: w[layer] for name, w in zip(_WEIGHT_NAMES, weights)})
        x_prompt, st_p = _prompt_layer(x_prompt, wts)
        x_sample, st_s = _sample_layer(x_sample, tuple(c[layer] for c in caches), page_table, wts)
        prompt_states.append(st_p)
        sample_states.append(st_s)
    new_prompt = [jnp.stack(rows) for rows in zip(*prompt_states)]
    new_sample = [jnp.stack(rows) for rows in zip(*sample_states)]
    return (x_prompt, x_sample, *new_prompt, *new_sample)
```

```python
import functools

import numpy as np
import jax
import jax.numpy as jnp
from jax import lax
from jax.experimental import pallas as pl
from jax.experimental.pallas import tpu as pltpu

F32 = jnp.float32
BF16 = jnp.bfloat16

D_MODEL = 1024
HEAD_DIM = 64
NSA_HEADS = 8
NSA_KV_HEADS = 2
NSA_GROUP = NSA_HEADS // NSA_KV_HEADS
FOX_HEADS = 8
CMP_BLOCK = 32
CMP_STRIDE = 16
CMP_HIDDEN = 2 * HEAD_DIM
SEL_BLOCK = 64
N_SEL = 16
WINDOW = 512
D_FF = 3 * D_MODEL
CONV_W = 3
PAGE_SIZE = 128
D_QA = NSA_HEADS * HEAD_DIM
D_KVA = NSA_KV_HEADS * HEAD_DIM
D_FOX = FOX_HEADS * HEAD_DIM
IN_SIZES = (D_QA, D_KVA, D_KVA, D_KVA, D_KVA, D_KVA, D_KVA, 3 * NSA_HEADS, D_FOX, D_FOX, D_FOX, FOX_HEADS,
            D_MODEL, D_MODEL)
ATTN_SCALE = HEAD_DIM ** -0.5
FORCED_SCORE = 1e6
NEG_INF = -1e30
TINY = 1e-30
EPS = 1e-6
LANES = 128
N_GATE = 3 * NSA_HEADS
VMEM_LIMIT = 56 * 1024 * 1024


def _dot(a, b):
    return jnp.dot(a, b, preferred_element_type=F32)


def _dot_nt(a, b):
    return lax.dot_general(a, b, (((1,), (1,)), ((), ())), preferred_element_type=F32)


def _params(*sem):
    return pltpu.CompilerParams(dimension_semantics=sem, vmem_limit_bytes=VMEM_LIMIT)


def _resident(shape):
    nd = len(shape)
    return pl.BlockSpec(shape, lambda *_: (0,) * nd, pipeline_mode=pl.Buffered(1))


def _lane_iota(shape):
    return lax.broadcasted_iota(jnp.int32, shape, len(shape) - 1)


def _row_iota(shape):
    return lax.broadcasted_iota(jnp.int32, shape, len(shape) - 2)


def _gelu(x):
    return 0.5 * x * (1.0 + jnp.tanh(0.7978845608028654 * (x + 0.044715 * (x * x * x))))


def _sigmoid(x):
    return 1.0 / (1.0 + jnp.exp(-x))


def _log_sigmoid(x):
    return -(jnp.maximum(-x, 0.0) + jnp.log1p(jnp.exp(-jnp.abs(x))))


def _softmax_rows(logits, mask):
    lg = jnp.where(mask, logits, NEG_INF)
    m = jnp.max(lg, axis=-1, keepdims=True)
    e = jnp.where(mask, jnp.exp(lg - m), 0.0)
    return e / jnp.maximum(jnp.sum(e, axis=-1, keepdims=True), TINY)


_C_QA = 0
_C_CK = _C_QA + D_QA
_C_CV = _C_CK + D_KVA
_C_FQ = _C_CV + D_KVA
_C_GA = _C_FQ + D_FOX
_C_GB = _C_GA + D_MODEL
_C_MISC = _C_GB + D_MODEL
_C_END = _C_MISC + LANES
_R_KV = 0
_R_FK = _R_KV + 6 * D_KVA
_R_FV = _R_FK + D_FOX
_R_FF = _R_FV + D_FOX
_R_END = _R_FF + 16


def _qa_perm():
    cols = []
    for p in range(NSA_GROUP):
        cols += list(range(p * HEAD_DIM, (p + 1) * HEAD_DIM))
        cols += list(range((NSA_GROUP + p) * HEAD_DIM, (NSA_GROUP + p + 1) * HEAD_DIM))
    return np.asarray(cols, np.int32)


def _permute_w_in(w_in):
    offs = [int(o) for o in np.concatenate([[0], np.cumsum(IN_SIZES)])]
    o_ck, o_sk, o_ng, o_fq, o_fk, o_ff, o_ga = offs[1], offs[3], offs[7], offs[8], offs[9], offs[11], offs[12]
    cols = np.concatenate([_qa_perm(), np.arange(o_ck, o_sk), np.arange(o_fq, o_fk), np.arange(o_ga, offs[-1]),
                           np.arange(o_ng, o_fq)])
    pad = jnp.zeros((w_in.shape[0], LANES - N_GATE), w_in.dtype)
    w_tok = jnp.concatenate([w_in[:, cols], pad], axis=1).astype(BF16)
    rows = np.concatenate([np.arange(o_ck, o_ng), np.arange(o_fk, o_ff), np.arange(o_ff, o_ga)])
    w_t = w_in.T
    w_feat = jnp.concatenate([w_t[rows], jnp.zeros((_R_END - len(rows), w_in.shape[0]), w_in.dtype)], axis=0)
    return w_tok, w_feat.astype(BF16)


def _head_mean_matrix():
    i = np.arange(D_QA)
    return jnp.asarray((i[:, None] // HEAD_DIM == i[None, :] // HEAD_DIM) / HEAD_DIM, BF16)


_PROJ_TOKEN_OUTS = (('qa', D_QA, BF16), ('ck', D_KVA, F32), ('cv', D_KVA, F32), ('fq', D_FOX, BF16),
                    ('ga', D_MODEL, F32), ('gb', D_MODEL, F32), ('gates', LANES, F32))
_PROJ_FEATURE_OUTS = (('ckT', D_KVA, F32), ('cvT', D_KVA, F32), ('skT', D_KVA, F32), ('svT', D_KVA, F32),
                      ('wkT', D_KVA, F32), ('wvT', D_KVA, F32), ('fkT', D_FOX, F32), ('fvT', D_FOX, F32),
                      ('logfT', FOX_HEADS, F32), ('skTb', D_KVA, BF16), ('svTb', D_KVA, BF16), ('wkTb', D_KVA, BF16),
                      ('wvTb', D_KVA, BF16), ('fkTb', D_FOX, BF16), ('fvTb', D_FOX, BF16))


def _proj_kernel(x_ref, gattn_ref, w_ref, wt_ref, m64_ref, gq_ref, gfq_ref, gsk_ref, gwk_ref, gfk_ref, bf_ref,
                 qa_ref, ck_ref, cv_ref, fq_ref, ga_ref, gb_ref, gates_ref,
                 ckT_ref, cvT_ref, skT_ref, svT_ref, wkT_ref, wvT_ref, fkT_ref, fvT_ref, logfT_ref,
                 skTb_ref, svTb_ref, wkTb_ref, wvTb_ref, fkTb_ref, fvTb_ref):
    x = x_ref[...]
    ms = jnp.mean(x * x, axis=-1, keepdims=True)
    xn = (x * lax.rsqrt(ms + EPS) * gattn_ref[...]).astype(BF16)

    def seg(a, width):
        return _dot(xn, w_ref[:, a:a + width])

    def seg_t(a, width):
        return _dot_nt(wt_ref[a:a + width, :], xn)

    def head_norm(z, g_ref):
        w = z.shape[-1]
        hm = _dot((z * z).astype(BF16), m64_ref[0:w, 0:w])
        return z * lax.rsqrt(hm + EPS) * g_ref[...]

    def store_t(z, f32_ref, bf_ref_, g_ref):
        for h in range(z.shape[0] // HEAD_DIM):
            rows = slice(h * HEAD_DIM, (h + 1) * HEAD_DIM)
            zh = z[rows]
            if g_ref is not None:
                zh = zh * lax.rsqrt(jnp.mean(zh * zh, axis=0, keepdims=True) + EPS) * g_ref[rows, :]
            f32_ref[0, rows, :] = zh
            if bf_ref_ is not None:
                bf_ref_[0, rows, :] = zh.astype(BF16)

    qa_ref[...] = head_norm(seg(_C_QA, D_QA), gq_ref).astype(BF16)
    ck_ref[...] = seg(_C_CK, D_KVA)
    cv_ref[...] = seg(_C_CV, D_KVA)
    fq_ref[...] = head_norm(seg(_C_FQ, D_FOX), gfq_ref).astype(BF16)
    ga_ref[...] = _sigmoid(seg(_C_GA, D_MODEL))
    gb_ref[...] = _sigmoid(seg(_C_GB, D_MODEL))
    gates_ref[...] = _sigmoid(seg(_C_MISC, LANES))

    store_t(seg_t(_R_KV, D_KVA), ckT_ref, None, None)
    store_t(seg_t(_R_KV + D_KVA, D_KVA), cvT_ref, None, None)
    store_t(seg_t(_R_KV + 2 * D_KVA, D_KVA), skT_ref, skTb_ref, gsk_ref)
    store_t(seg_t(_R_KV + 3 * D_KVA, D_KVA), svT_ref, svTb_ref, None)
    store_t(seg_t(_R_KV + 4 * D_KVA, D_KVA), wkT_ref, wkTb_ref, gwk_ref)
    store_t(seg_t(_R_KV + 5 * D_KVA, D_KVA), wvT_ref, wvTb_ref, None)
    store_t(seg_t(_R_FK, D_FOX), fkT_ref, fkTb_ref, gfk_ref)
    store_t(seg_t(_R_FV, D_FOX), fvT_ref, fvTb_ref, None)
    logfT_ref[0] = _log_sigmoid(seg_t(_R_FF, _R_END - _R_FF)[0:FOX_HEADS] + bf_ref[...])


def _project(x2d, wts, nb, t, tm):
    assert t % tm == 0
    nt = t // tm
    row = lambda w: pl.BlockSpec((tm, w), lambda i: (i, 0))
    feat = lambda w: pl.BlockSpec((1, w, tm), lambda i: (i // nt, 0, i % nt))
    small = [wts['g_attn'], wts['w_tok'], wts['w_feat'], wts['m64'], wts['gq'], wts['gfq'], wts['gsk_col'],
             wts['gwk_col'], wts['gfk_col'], wts['bf_col']]
    res = pl.pallas_call(
        _proj_kernel,
        grid=(nb * nt,),
        in_specs=[row(D_MODEL)] + [_resident(a.shape) for a in small],
        out_specs=[row(w) for _, w, _ in _PROJ_TOKEN_OUTS] + [feat(w) for _, w, _ in _PROJ_FEATURE_OUTS],
        out_shape=[jax.ShapeDtypeStruct((nb * t, w), dt) for _, w, dt in _PROJ_TOKEN_OUTS]
        + [jax.ShapeDtypeStruct((nb, w, t), dt) for _, w, dt in _PROJ_FEATURE_OUTS],
        compiler_params=_params("parallel"),
        name="proj",
    )(x2d, *small)
    return dict(zip([n for n, _, _ in _PROJ_TOKEN_OUTS + _PROJ_FEATURE_OUTS], res))


CHUNK_LANES = CMP_STRIDE * D_KVA
CHUNKS_PER_PAGE = PAGE_SIZE // CMP_STRIDE


def _cmp_weights(pe, w1, b1, w2):
    eye = jnp.eye(NSA_KV_HEADS, dtype=w1.dtype)
    big = lambda w: jnp.einsum('pdh,gk->pgdkh', w, eye).reshape(CHUNK_LANES, NSA_KV_HEADS * CMP_HIDDEN).astype(BF16)
    flat = lambda e: jnp.broadcast_to(e[:, None, :], (CMP_STRIDE, NSA_KV_HEADS, HEAD_DIM)).reshape(1, CHUNK_LANES)
    w2bd = jnp.einsum('he,gk->ghke', w2, eye).reshape(NSA_KV_HEADS * CMP_HIDDEN, D_KVA).astype(BF16)
    return dict(w1lo=big(w1[:CMP_STRIDE]), w1hi=big(w1[CMP_STRIDE:]), pelo=flat(pe[:CMP_STRIDE]),
                pehi=flat(pe[CMP_STRIDE:]), b1=jnp.tile(b1, NSA_KV_HEADS)[None, :], w2=w2bd)


def _cmp_lh_body(x, pelo, pehi, w1lo, w1hi):
    lo = _dot((x + pelo).astype(BF16), w1lo)
    hi = _dot((x + pehi).astype(BF16), w1hi)
    return jnp.concatenate([lo, hi], axis=1)


def _cmp_lh_kernel(*refs, n_blocks):
    xk = refs[:n_blocks]
    xv = refs[n_blocks:2 * n_blocks]
    pk = refs[2 * n_blocks:2 * n_blocks + 4]
    pv = refs[2 * n_blocks + 4:2 * n_blocks + 8]
    ok_ref, ov_ref = refs[2 * n_blocks + 8:]

    def rows(blocks):
        parts = [b[...].reshape(b.shape[-2], b.shape[-1]) for b in blocks]
        return parts[0] if len(parts) == 1 else jnp.concatenate(parts, axis=0)

    ok_ref[...] = _cmp_lh_body(rows(xk), pk[0][...], pk[1][...], pk[2][...], pk[3][...])
    ov_ref[...] = _cmp_lh_body(rows(xv), pv[0][...], pv[1][...], pv[2][...], pv[3][...])


def _cmp_lh_dense(xk, xv, cwk, cwv, tm):
    n = xk.shape[0]
    assert n % tm == 0
    row = lambda w: pl.BlockSpec((tm, w), lambda i: (i, 0))
    par = [cwk['pelo'], cwk['pehi'], cwk['w1lo'], cwk['w1hi'], cwv['pelo'], cwv['pehi'], cwv['w1lo'], cwv['w1hi']]
    wlh = 2 * NSA_KV_HEADS * CMP_HIDDEN
    return pl.pallas_call(
        functools.partial(_cmp_lh_kernel, n_blocks=1),
        grid=(n // tm,),
        in_specs=[row(CHUNK_LANES), row(CHUNK_LANES)] + [_resident(a.shape) for a in par],
        out_specs=[row(wlh), row(wlh)],
        out_shape=[jax.ShapeDtypeStruct((n, wlh), F32)] * 2,
        compiler_params=_params("parallel"),
        name="cmp_lh",
    )(xk, xv, *par)


def _cmp_lh_paged(pool_k, pool_v, page_table, cwk, cwv, pages_per_step):
    nb, n_pages = page_table.shape
    assert n_pages % pages_per_step == 0
    steps = n_pages // pages_per_step

    def page_spec(u):
        return pl.BlockSpec((1, CHUNKS_PER_PAGE, CHUNK_LANES),
                            lambda b, j, pt: (pt[b, j * pages_per_step + u], 0, 0))

    par = [cwk['pelo'], cwk['pehi'], cwk['w1lo'], cwk['w1hi'], cwv['pelo'], cwv['pehi'], cwv['w1lo'], cwv['w1hi']]
    wlh = 2 * NSA_KV_HEADS * CMP_HIDDEN
    rows = pages_per_step * CHUNKS_PER_PAGE
    out_spec = pl.BlockSpec((rows, wlh), lambda b, j, pt: (b * steps + j, 0))
    const = lambda a: pl.BlockSpec(a.shape, lambda b, j, pt: (0,) * a.ndim, pipeline_mode=pl.Buffered(1))
    def kern(pt_ref, *refs):
        _cmp_lh_kernel(*refs, n_blocks=pages_per_step)

    return pl.pallas_call(
        kern,
        grid_spec=pltpu.PrefetchScalarGridSpec(
            num_scalar_prefetch=1,
            grid=(nb, steps),
            in_specs=[page_spec(u) for u in range(pages_per_step)] * 2 + [const(a) for a in par],
            out_specs=[out_spec, out_spec],
        ),
        out_shape=[jax.ShapeDtypeStruct((nb * n_pages * CHUNKS_PER_PAGE, wlh), F32)] * 2,
        compiler_params=_params("parallel", "parallel"),
        name="cmp_lh_paged",
    )(page_table, *([pool_k] * pages_per_step), *([pool_v] * pages_per_step), *par)


def _cmp_finish_kernel(lhk_ref, lhv_ref, b1k_ref, w2k_ref, b1v_ref, w2v_ref, m64_ref, gk_ref, kc_ref, vc_ref):
    def finish(lh_ref, b1_ref, w2_ref):
        lh = lh_ref[...]
        nc = lh.shape[0]
        half = lh.shape[1] // 2
        hi_next = pltpu.roll(lh[:, half:], nc - 1, 0)
        h = _gelu(lh[:, :half] + hi_next + b1_ref[...])
        return _dot(h.astype(BF16), w2_ref[...])

    kc = finish(lhk_ref, b1k_ref, w2k_ref)
    hm = _dot((kc * kc).astype(BF16), m64_ref[0:D_KVA, 0:D_KVA])
    kc_ref[...] = (kc * lax.rsqrt(hm + EPS) * gk_ref[...]).astype(BF16)
    vc_ref[...] = finish(lhv_ref, b1v_ref, w2v_ref).astype(BF16)


def _cmp_finish(lhk, lhv, cwk, cwv, m64, gk, nc):
    n = lhk.shape[0]
    nb = n // nc
    seq = lambda w: pl.BlockSpec((nc, w), lambda b: (b, 0))
    par = [cwk['b1'], cwk['w2'], cwv['b1'], cwv['w2'], m64, gk]
    return pl.pallas_call(
        _cmp_finish_kernel,
        grid=(nb,),
        in_specs=[seq(lhk.shape[1]), seq(lhv.shape[1])] + [_resident(a.shape) for a in par],
        out_specs=[seq(D_KVA), seq(D_KVA)],
        out_shape=[jax.ShapeDtypeStruct((n, D_KVA), BF16)] * 2,
        compiler_params=_params("parallel"),
        name="cmp_finish",
    )(lhk, lhv, *par)


def _alibi_slope(g, r):
    return 2.0 ** -(NSA_GROUP * g + r + 1)


def _stack_heads(q):
    return jnp.concatenate([q[:, p * LANES:(p + 1) * LANES] for p in range(NSA_GROUP)], axis=0)


def _group_lanes(x, g):
    return jnp.where((_lane_iota(x.shape) // HEAD_DIM) == g, x, jnp.zeros_like(x))


def _unstack_heads(o0, o1, tq):
    lo = _lane_iota((tq, LANES)) < HEAD_DIM
    return jnp.concatenate([jnp.where(lo, o0[p * tq:(p + 1) * tq], o1[p * tq:(p + 1) * tq])
                            for p in range(NSA_GROUP)], axis=1)


def _cmp_attn_kernel(q_ref, kc_ref, vc_ref, o_ref, sel_ref, idx_ref, *, tq, n_slc, pos0):
    nck = kc_ref.shape[0]
    nsl = sel_ref.shape[-1]
    qpos0 = pos0 + pl.program_id(1) * tq
    qs = _stack_heads(q_ref[...])
    kc = kc_ref[...]
    vc = vc_ref[...]
    dist = (qpos0 + _row_iota((tq, nck))) - (_lane_iota((tq, nck)) * CMP_STRIDE + (CMP_BLOCK - 1))
    mask = dist >= 0
    distf = dist.astype(F32)
    c_start = _row_iota((nck, nsl)) * CMP_STRIDE
    s_start = _lane_iota((nck, nsl)) * SEL_BLOCK
    overlap = jnp.where((c_start < s_start + SEL_BLOCK) & (c_start + CMP_BLOCK > s_start), 1.0, 0.0).astype(BF16)
    blk = _lane_iota((tq, nsl))
    blkf = blk.astype(F32)
    qp = qpos0 + _row_iota((tq, nsl))
    cur = lax.shift_right_logical(qp, 6)
    forced = (blk == 0) | (blk == cur) | (blk == cur - 1)
    avail = blk * SEL_BLOCK <= qp
    slot = _lane_iota((tq, LANES))
    outs = []
    for g in range(NSA_KV_HEADS):
        s = _dot_nt(qs, _group_lanes(kc, g))
        ps = [_softmax_rows(s[r * tq:(r + 1) * tq] - _alibi_slope(g, r) * distf, mask) for r in range(NSA_GROUP)]
        outs.append(_dot(jnp.concatenate(ps, axis=0).astype(BF16), vc))
        psum = (ps[0] + ps[1]) + (ps[2] + ps[3])
        hi = psum.astype(BF16)
        lo = (psum - hi.astype(F32)).astype(BF16)
        p_slc = _dot(hi, overlap) + _dot(lo, overlap)
        score = jnp.where(forced, FORCED_SCORE, jnp.where(avail, p_slc, -1.0))
        score = jnp.where(blk < n_slc, score, -2.0)
        sel = jnp.zeros((tq, nsl), F32)
        idx = jnp.zeros((tq, LANES), F32)
        for it in range(N_SEL):
            m = jnp.max(score, axis=-1, keepdims=True)
            j = jnp.min(jnp.where(score == m, blkf, 1e9), axis=-1, keepdims=True)
            hit = blkf == j
            sel = jnp.where(hit, 1.0, sel)
            score = jnp.where(hit, -3.0, score)
            idx = jnp.where(slot == it, j, idx)
        sel_ref[0, g] = sel.astype(BF16)
        idx_ref[0, g] = idx.astype(jnp.int32)
    o_ref[...] = _unstack_heads(outs[0], outs[1], tq)


def _cmp_attend(q2d, kc2d, vc2d, nb, t, tq, nck, n_slc, pos0):
    nsl = -(-n_slc // LANES) * LANES
    nt = t // tq
    return pl.pallas_call(
        functools.partial(_cmp_attn_kernel, tq=tq, n_slc=n_slc, pos0=pos0),
        grid=(nb, nt),
        in_specs=[pl.BlockSpec((tq, D_QA), lambda b, i: (b * nt + i, 0)),
                  pl.BlockSpec((nck, D_KVA), lambda b, i: (b, 0)),
                  pl.BlockSpec((nck, D_KVA), lambda b, i: (b, 0))],
        out_specs=[pl.BlockSpec((tq, D_QA), lambda b, i: (b * nt + i, 0)),
                   pl.BlockSpec((1, NSA_KV_HEADS, tq, nsl), lambda b, i: (b, 0, i, 0)),
                   pl.BlockSpec((1, NSA_KV_HEADS, tq, LANES), lambda b, i: (b, 0, i, 0))],
        out_shape=[jax.ShapeDtypeStruct((nb * t, D_QA), F32),
                   jax.ShapeDtypeStruct((nb, NSA_KV_HEADS, t, nsl), BF16),
                   jax.ShapeDtypeStruct((nb, NSA_KV_HEADS, t, LANES), jnp.int32)],
        compiler_params=_params("parallel", "parallel"),
        name="cmp_attn",
    )(q2d, kc2d, vc2d)


def _sel_prompt_kernel(q_ref, k_ref, v_ref, sel_ref, o_ref, m_ref, l_ref, acc_ref, *, tq, tk):
    i = pl.program_id(1)
    nsl = sel_ref.shape[-1]
    qpos0 = i * tq
    qs = _stack_heads(q_ref[...])
    n_kt = ((i + 1) * tq + tk - 1) // tk
    outs = []
    for g in range(NSA_KV_HEADS):
        m_ref[...] = jnp.full(m_ref.shape, NEG_INF, F32)
        l_ref[...] = jnp.zeros(l_ref.shape, F32)
        acc_ref[...] = jnp.zeros(acc_ref.shape, F32)
        sel_g = sel_ref[0, g]

        qg = _group_lanes(qs, g)

        def body(kt, carry):
            k0 = pl.multiple_of(kt * tk, tk)
            vt = v_ref[0, :, pl.ds(k0, tk)]
            s = _dot(qg, k_ref[0, :, pl.ds(k0, tk)])
            kpos = k0 + _lane_iota((tq, tk))
            dist = (qpos0 + _row_iota((tq, tk))) - kpos
            expand = jnp.where(lax.shift_right_logical(k0 + _lane_iota((nsl, tk)), 6) == _row_iota((nsl, tk)),
                               1.0, 0.0).astype(BF16)
            mask = (_dot(sel_g, expand) > 0.5) & (dist >= 0)
            distf = dist.astype(F32)
            ps, alphas = [], []
            for r in range(NSA_GROUP):
                rows = slice(r * tq, (r + 1) * tq)
                lg = jnp.where(mask, s[rows] - _alibi_slope(g, r) * distf, NEG_INF)
                m_old = m_ref[rows, :]
                m_new = jnp.maximum(m_old, jnp.max(lg, axis=-1, keepdims=True))
                p = jnp.where(mask, jnp.exp(lg - m_new), 0.0)
                alpha = jnp.exp(m_old - m_new)
                l_ref[rows, :] = alpha * l_ref[rows, :] + jnp.sum(p, axis=-1, keepdims=True)
                m_ref[rows, :] = m_new
                ps.append(p.astype(BF16))
                alphas.append(alpha)
            pv = _dot_nt(jnp.concatenate(ps, axis=0), vt)
            acc_ref[...] = jnp.concatenate(alphas, axis=0) * acc_ref[...] + pv
            return carry

        lax.fori_loop(0, n_kt, body, 0)
        outs.append(acc_ref[...] / jnp.maximum(l_ref[...], TINY))
    o_ref[...] = _unstack_heads(outs[0], outs[1], tq)


def _sel_prompt(q2d, k_t, v_t, sel, nb, t, tq, tk):
    nt = t // tq
    nsl = sel.shape[-1]
    return pl.pallas_call(
        functools.partial(_sel_prompt_kernel, tq=tq, tk=tk),
        grid=(nb, nt),
        in_specs=[pl.BlockSpec((tq, D_QA), lambda b, i: (b * nt + i, 0)),
                  pl.BlockSpec((1, D_KVA, t), lambda b, i: (b, 0, 0)),
                  pl.BlockSpec((1, D_KVA, t), lambda b, i: (b, 0, 0)),
                  pl.BlockSpec((1, NSA_KV_HEADS, tq, nsl), lambda b, i: (b, 0, i, 0))],
        out_specs=pl.BlockSpec((tq, D_QA), lambda b, i: (b * nt + i, 0)),
        out_shape=jax.ShapeDtypeStruct((nb * t, D_QA), F32),
        scratch_shapes=[pltpu.VMEM((NSA_GROUP * tq, 1), F32), pltpu.VMEM((NSA_GROUP * tq, 1), F32),
                        pltpu.VMEM((NSA_GROUP * tq, LANES), F32)],
        compiler_params=_params("parallel", "parallel"),
        name="sel_prompt",
    )(q2d, k_t, v_t, sel)


def _window_core(q, k_t, v_t, qpos0, kpos0, tq):
    nk = k_t.shape[1]
    qs = _stack_heads(q)
    kpos = kpos0 + _lane_iota((tq, nk))
    dist = (qpos0 + _row_iota((tq, nk))) - kpos
    mask = (dist >= 0) & (dist <= WINDOW) & (kpos >= 0)
    distf = dist.astype(F32)
    outs = []
    for g in range(NSA_KV_HEADS):
        s = _dot(_group_lanes(qs, g), k_t)
        ps = [_softmax_rows(s[r * tq:(r + 1) * tq] - _alibi_slope(g, r) * distf, mask).astype(BF16)
              for r in range(NSA_GROUP)]
        outs.append(_dot_nt(jnp.concatenate(ps, axis=0), v_t))
    return _unstack_heads(outs[0], outs[1], tq)


def _win_prompt_kernel(q_ref, k_ref, v_ref, o_ref, *, tq, span):
    i = pl.program_id(1)
    k0 = pl.multiple_of(jnp.maximum(i * tq + tq - span, 0), tq)
    o_ref[...] = _window_core(q_ref[...], k_ref[0, :, pl.ds(k0, span)], v_ref[0, :, pl.ds(k0, span)], i * tq, k0, tq)


def _win_prompt(q2d, k_t, v_t, nb, t, tq):
    nt = t // tq
    span = min(tq + WINDOW, t)
    return pl.pallas_call(
        functools.partial(_win_prompt_kernel, tq=tq, span=span),
        grid=(nb, nt),
        in_specs=[pl.BlockSpec((tq, D_QA), lambda b, i: (b * nt + i, 0)),
                  pl.BlockSpec((1, D_KVA, t), lambda b, i: (b, 0, 0)),
                  pl.BlockSpec((1, D_KVA, t), lambda b, i: (b, 0, 0))],
        out_specs=pl.BlockSpec((tq, D_QA), lambda b, i: (b * nt + i, 0)),
        out_shape=jax.ShapeDtypeStruct((nb * t, D_QA), F32),
        compiler_params=_params("parallel", "parallel"),
        name="win_prompt",
    )(q2d, k_t, v_t)


def _split3(x):
    a = x.astype(BF16)
    r = x - a.astype(F32)
    b = r.astype(BF16)
    c = (r - b.astype(F32)).astype(BF16)
    return a, b, c


def _dot3(x, m):
    a, b, c = _split3(x)
    return _dot(a, m) + (_dot(b, m) + _dot(c, m))


def _fox_bias_kernel(lf_ref, o_ref, *, t):
    upto = jnp.where(_row_iota((LANES, LANES)) <= _lane_iota((LANES, LANES)), 1.0, 0.0).astype(BF16)
    carry = jnp.zeros((FOX_HEADS, 1), F32)
    for j in range(t // LANES):
        cols = slice(j * LANES, (j + 1) * LANES)
        c = _dot3(lf_ref[0, :, cols], upto) + carry
        carry = c[:, LANES - 1:LANES]
        o_ref[0, :, cols] = -c


def _fox_bias(logf_t, nb, t):
    blk = pl.BlockSpec((1, FOX_HEADS, t), lambda b: (b, 0, 0))
    return pl.pallas_call(
        functools.partial(_fox_bias_kernel, t=t),
        grid=(nb,),
        in_specs=[blk],
        out_specs=blk,
        out_shape=jax.ShapeDtypeStruct((nb, FOX_HEADS, t), F32),
        compiler_params=_params("parallel"),
        name="fox_bias",
    )(logf_t)


def _fox_prompt_kernel(q_ref, k_ref, v_ref, b_ref, o_ref, m_ref, l_ref, acc_ref, *, tq, tk):
    i = pl.program_id(1)
    qpos0 = i * tq
    n_kt = ((i + 1) * tq + tk - 1) // tk
    lo_lanes = _lane_iota((tq, LANES)) < HEAD_DIM
    for p in range(FOX_HEADS // 2):
        qp = q_ref[:, p * LANES:(p + 1) * LANES]
        q2 = jnp.concatenate([jnp.where(lo_lanes, qp, jnp.zeros_like(qp)),
                              jnp.where(lo_lanes, jnp.zeros_like(qp), qp)], axis=0)
        m_ref[...] = jnp.full(m_ref.shape, NEG_INF, F32)
        l_ref[...] = jnp.zeros(l_ref.shape, F32)
        acc_ref[...] = jnp.zeros(acc_ref.shape, F32)

        def body(kt, carry):
            k0 = pl.multiple_of(kt * tk, tk)
            s = _dot(q2, k_ref[0, p * LANES:(p + 1) * LANES, pl.ds(k0, tk)])
            mask = (k0 + _lane_iota((tq, tk))) <= (qpos0 + _row_iota((tq, tk)))
            ps, alphas = [], []
            for hh in range(2):
                rows = slice(hh * tq, (hh + 1) * tq)
                bias = b_ref[0, 2 * p + hh:2 * p + hh + 1, pl.ds(k0, tk)]
                lg = jnp.where(mask, s[rows] + bias, NEG_INF)
                m_old = m_ref[rows, :]
                m_new = jnp.maximum(m_old, jnp.max(lg, axis=-1, keepdims=True))
                pr = jnp.where(mask, jnp.exp(lg - m_new), 0.0)
                alpha = jnp.exp(m_old - m_new)
                l_ref[rows, :] = alpha * l_ref[rows, :] + jnp.sum(pr, axis=-1, keepdims=True)
                m_ref[rows, :] = m_new
                ps.append(pr.astype(BF16))
                alphas.append(alpha)
            pv = _dot_nt(jnp.concatenate(ps, axis=0), v_ref[0, p * LANES:(p + 1) * LANES, pl.ds(k0, tk)])
            acc_ref[...] = jnp.concatenate(alphas, axis=0) * acc_ref[...] + pv
            return carry

        lax.fori_loop(0, n_kt, body, 0)
        o = acc_ref[...] / jnp.maximum(l_ref[...], TINY)
        o_ref[:, p * LANES:(p + 1) * LANES] = jnp.where(lo_lanes, o[:tq], o[tq:])


def _fox_prompt(q2d, k_t, v_t, bias, nb, t, tq, tk):
    nt = t // tq
    return pl.pallas_call(
        functools.partial(_fox_prompt_kernel, tq=tq, tk=tk),
        grid=(nb, nt),
        in_specs=[pl.BlockSpec((tq, D_FOX), lambda b, i: (b * nt + i, 0)),
                  pl.BlockSpec((1, D_FOX, t), lambda b, i: (b, 0, 0)),
                  pl.BlockSpec((1, D_FOX, t), lambda b, i: (b, 0, 0)),
                  pl.BlockSpec((1, FOX_HEADS, t), lambda b, i: (b, 0, 0))],
        out_specs=pl.BlockSpec((tq, D_FOX), lambda b, i: (b * nt + i, 0)),
        out_shape=jax.ShapeDtypeStruct((nb * t, D_FOX), F32),
        scratch_shapes=[pltpu.VMEM((2 * tq, 1), F32), pltpu.VMEM((2 * tq, 1), F32),
                        pltpu.VMEM((2 * tq, LANES), F32)],
        compiler_params=_params("parallel", "parallel"),
        name="fox_prompt",
    )(q2d, k_t, v_t, bias)


def _merge_kernel(x_ref, oc_ref, os_ref, ow_ref, misc_ref, of_ref, ga_ref, gb_ref, woa_ref, wob_ref, wout_ref, h_ref):
    tm = x_ref.shape[0]
    misc = misc_ref[...]
    lo_lanes = _lane_iota((tm, LANES)) < HEAD_DIM

    def gate(head, j):
        c = 3 * head + j
        return jnp.broadcast_to(misc[:, c:c + 1], (tm, LANES))

    chunks = []
    for p in range(NSA_GROUP):
        cols = slice(p * LANES, (p + 1) * LANES)
        acc = None
        for j, o_ref in enumerate((oc_ref, os_ref, ow_ref)):
            term = jnp.where(lo_lanes, gate(p, j), gate(NSA_GROUP + p, j)) * o_ref[:, cols]
            acc = term if acc is None else acc + term
        chunks.append(acc.astype(BF16))
    ya = _dot(jnp.concatenate(chunks, axis=1), woa_ref[...])
    yb = _dot(of_ref[...].astype(BF16), wob_ref[...])
    mix = (ga_ref[...] * ya + gb_ref[...] * yb).astype(BF16)
    h_ref[...] = x_ref[...] + _dot(mix, wout_ref[...])


def _merge(x2d, o_cmp, o_sel, o_win, misc, o_fox, ga, gb, wts, tm):
    n = x2d.shape[0]
    assert n % tm == 0
    row = lambda w: pl.BlockSpec((tm, w), lambda i: (i, 0))
    par = [wts['w_o_a'], wts['w_o_b'], wts['w_out']]
    return pl.pallas_call(
        _merge_kernel,
        grid=(n // tm,),
        in_specs=[row(D_MODEL), row(D_QA), row(D_QA), row(D_QA), row(LANES), row(D_FOX), row(D_MODEL), row(D_MODEL)]
        + [_resident(a.shape) for a in par],
        out_specs=row(D_MODEL),
        out_shape=jax.ShapeDtypeStruct((n, D_MODEL), F32),
        compiler_params=_params("parallel"),
        name="merge",
    )(x2d, o_cmp, o_sel, o_win, misc, o_fox, ga, gb, *par)


FFN_CHUNK = 512


def _ffn_kernel(h_ref, p1_ref, p2_ref, g_ref, wup_ref, wgate_ref, wdown_ref, cw_ref, cb_ref, y_ref, tail_ref,
                carry_ref, *, tm, t_seq, carry):
    i = pl.program_id(0)
    h = h_ref[...]
    ms = jnp.mean(h * h, axis=-1, keepdims=True)
    hn = (h * lax.rsqrt(ms + EPS) * g_ref[...]).astype(BF16)
    t_in_seq = (i * tm + _row_iota((tm, FFN_CHUNK))) % t_seq
    acc = jnp.zeros((tm, D_MODEL), F32)
    for f in range(D_FF // FFN_CHUNK):
        cols = slice(f * FFN_CHUNK, (f + 1) * FFN_CHUNK)
        u = _dot(hn, wup_ref[:, cols])
        gt = _dot(hn, wgate_ref[:, cols])
        r1 = pltpu.roll(u, 1, 0)
        r2 = pltpu.roll(u, 2, 0)
        if carry:
            prev = carry_ref[:, cols]
            first = (i * tm) % t_seq == 0
            prev = jnp.where(first, jnp.zeros_like(prev), prev)
            row = _row_iota((tm, FFN_CHUNK))
            r1 = jnp.where(row == 0, prev[7:8, :], r1)
            r2 = jnp.where(row == 0, prev[6:7, :], jnp.where(row == 1, prev[7:8, :], r2))
            carry_ref[:, cols] = u[tm - 8:tm, :]
        else:
            r1 = jnp.where(t_in_seq >= 1, r1, p1_ref[:, cols])
            r2 = jnp.where(t_in_seq >= 2, r2, p2_ref[:, cols])
        uc = cb_ref[:, cols] + cw_ref[0:1, cols] * r2 + cw_ref[1:2, cols] * r1 + cw_ref[2:3, cols] * u
        acc = acc + _dot((_gelu(uc) * gt).astype(BF16), wdown_ref[cols, :])
        tail_ref[:, cols] = u[tm - tail_ref.shape[0]:tm, :]
    y_ref[...] = h + acc


def _ffn(h2d, p1, p2, wts, tm, t_seq, tail_rows):
    n = h2d.shape[0]
    assert n % tm == 0 and tail_rows % 8 == 0
    carry = p1 is None
    assert (t_seq % tm == 0) if carry else (tm % t_seq == 0)
    row = lambda w: pl.BlockSpec((tm, w), lambda i: (i, 0))
    par = [wts['g_ffn'], wts['w_up'], wts['w_gate'], wts['w_down'], wts['conv_w'], wts['conv_b']]
    if carry:
        kern = lambda h_ref, *rest, **kw: _ffn_kernel(h_ref, None, None, *rest, **kw)
        acts, act_specs = [h2d], [row(D_MODEL)]
    else:
        kern = _ffn_kernel
        acts, act_specs = [h2d, p1, p2], [row(D_MODEL), row(D_FF), row(D_FF)]
    return pl.pallas_call(
        functools.partial(kern, tm=tm, t_seq=t_seq, carry=carry),
        grid=(n // tm,),
        in_specs=act_specs + [_resident(a.shape) for a in par],
        out_specs=[row(D_MODEL), pl.BlockSpec((tail_rows, D_FF), lambda i: (i, 0))],
        out_shape=[jax.ShapeDtypeStruct((n, D_MODEL), F32),
                   jax.ShapeDtypeStruct((n // tm * tail_rows, D_FF), F32)],
        scratch_shapes=[pltpu.VMEM((8, D_FF), F32)],
        compiler_params=_params("arbitrary"),
        name="ffn",
    )(*acts, *par)


def _prep_weights(p):
    tile = lambda g, n: jnp.tile(g.astype(F32), n)[None, :]
    col = lambda g, n: jnp.tile(g.astype(F32), n)[:, None]
    w_tok, w_feat = _permute_w_in(p['w_in'])
    return dict(
        g_attn=p['norm_attn_g'].astype(F32)[None, :],
        w_tok=w_tok,
        w_feat=w_feat,
        m64=_head_mean_matrix(),
        gq=tile(p['q_norm_a_g'], NSA_HEADS) * ATTN_SCALE,
        gfq=tile(p['q_norm_b_g'], FOX_HEADS) * ATTN_SCALE,
        gkc=tile(p['k_norm_cmp_g'], NSA_KV_HEADS),
        gsk_col=col(p['k_norm_sel_g'], NSA_KV_HEADS),
        gwk_col=col(p['k_norm_win_g'], NSA_KV_HEADS),
        gfk_col=col(p['k_norm_b_g'], FOX_HEADS),
        bf_col=p['b_forget'].astype(F32)[:, None],
        cmpk=_cmp_weights(p['cmp_pe_k'], p['cmp_w1_k'], p['cmp_b1_k'], p['cmp_w2_k']),
        cmpv=_cmp_weights(p['cmp_pe_v'], p['cmp_w1_v'], p['cmp_b1_v'], p['cmp_w2_v']),
        w_o_a=p['w_o_a'][_qa_perm()].astype(BF16),
        w_o_b=p['w_o_b'].astype(BF16),
        w_out=p['w_out'].astype(BF16),
        g_ffn=p['norm_ffn_g'].astype(F32)[None, :],
        w_up=p['w_up'].astype(BF16),
        w_gate=p['w_gate'].astype(BF16),
        w_down=p['w_down'].astype(BF16),
        conv_w=p['conv_w'].astype(F32),
        conv_b=p['conv_b'].astype(F32)[None, :],
    )


Q_PAD = 16


SEL_HALVES = PAGE_SIZE // SEL_BLOCK


def _sel_sample_kernel(pt_ref, ix_ref, q_ref, *refs, n_cached, past):
    kb, vb = refs[:N_SEL], refs[N_SEL:2 * N_SEL]
    knew_ref, vnew_ref, o_ref = refs[2 * N_SEL:]
    b, t, g = pl.program_id(0), pl.program_id(1), pl.program_id(2)
    nk = (N_SEL + 1) * PAGE_SIZE
    kcat = jnp.concatenate([r[0, 0] for r in kb] + [knew_ref[0, 0]], axis=1).astype(BF16)
    vcat = jnp.concatenate([r[0, 0] for r in vb] + [vnew_ref[0, 0]], axis=1).astype(BF16)
    lane = _lane_iota((1, nk))
    slot = lane // PAGE_SIZE
    col = lane % PAGE_SIZE
    kpos = col + past
    valid = slot == N_SEL
    for j in range(N_SEL):
        blk = ix_ref[b, g * (ix_ref.shape[1] // NSA_KV_HEADS) + t * N_SEL + j]
        here = slot == j
        kpos = jnp.where(here, col + (blk // SEL_HALVES) * PAGE_SIZE, kpos)
        valid = valid | (here & (blk < n_cached) & ((col // SEL_BLOCK) == (blk % SEL_HALVES)))
    qpos = past + t
    dist = qpos - kpos
    mask = valid & (dist >= 0)
    row = _row_iota((Q_PAD, 1))
    slope = jnp.where(row == 0, 0.5, jnp.where(row == 1, 0.25, jnp.where(row == 2, 0.125, 0.0625)))
    slope = slope * jnp.where(g == 0, 1.0, 2.0 ** -NSA_GROUP)
    s = _dot(q_ref[0, 0, 0], kcat)
    p = _softmax_rows(s - slope * dist.astype(F32), jnp.broadcast_to(mask, s.shape))
    o_ref[0, 0, 0] = _dot_nt(p.astype(BF16), vcat)


def _sel_sample(qsel, pool_k, pool_v, knew, vnew, page_table, ix, past):
    nb, nt = qsel.shape[:2]
    n_cached = past // SEL_BLOCK

    def page_spec(j):
        def imap(b, t, g, pt, ixr):
            blk = jnp.minimum(ixr[b, g * (nt * N_SEL) + t * N_SEL + j], n_cached - 1)
            return (pt[b, blk // SEL_HALVES], g, 0, 0)
        return pl.BlockSpec((1, 1, HEAD_DIM, PAGE_SIZE), imap)

    new_spec = pl.BlockSpec((1, 1, HEAD_DIM, PAGE_SIZE), lambda b, t, g, pt, ixr: (b, g, 0, 0))
    q_spec = pl.BlockSpec((1, 1, 1, Q_PAD, HEAD_DIM), lambda b, t, g, pt, ixr: (b, t, g, 0, 0))
    return pl.pallas_call(
        functools.partial(_sel_sample_kernel, n_cached=n_cached, past=past),
        grid_spec=pltpu.PrefetchScalarGridSpec(
            num_scalar_prefetch=2,
            grid=(nb, nt, NSA_KV_HEADS),
            in_specs=[q_spec] + [page_spec(j) for j in range(N_SEL)] * 2 + [new_spec, new_spec],
            out_specs=q_spec,
        ),
        out_shape=jax.ShapeDtypeStruct((nb, nt, NSA_KV_HEADS, Q_PAD, HEAD_DIM), F32),
        compiler_params=_params("parallel", "parallel", "parallel"),
        name="sel_sample",
    )(page_table, ix, qsel, *([pool_k] * N_SEL), *([pool_v] * N_SEL), knew, vnew)


def _win_sample_kernel(q_ref, k_ref, v_ref, o_ref, *, past, w_buf):
    o_ref[...] = _window_core(q_ref[...], k_ref[0].astype(BF16), v_ref[0].astype(BF16), past, past - w_buf, Q_PAD)


def _win_sample(qpad, k_all, v_all, past, w_buf):
    nb, _, nk = k_all.shape
    return pl.pallas_call(
        functools.partial(_win_sample_kernel, past=past, w_buf=w_buf),
        grid=(nb,),
        in_specs=[pl.BlockSpec((Q_PAD, D_QA), lambda b: (b, 0)),
                  pl.BlockSpec((1, D_KVA, nk), lambda b: (b, 0, 0)),
                  pl.BlockSpec((1, D_KVA, nk), lambda b: (b, 0, 0))],
        out_specs=pl.BlockSpec((Q_PAD, D_QA), lambda b: (b, 0)),
        out_shape=jax.ShapeDtypeStruct((nb * Q_PAD, D_QA), F32),
        compiler_params=_params("parallel"),
        name="win_sample",
    )(qpad, k_all, v_all)


def _fox_decode_kernel(pt_ref, q_ref, *refs, pps, n_new):
    kp, vp, lp = refs[:pps], refs[pps:2 * pps], refs[2 * pps:3 * pps]
    knew_ref, vnew_ref, lnew_ref, o_ref, m_ref, l_ref, acc_ref, carry_ref = refs[3 * pps:]
    j = pl.program_id(1)
    rows = n_new * FOX_HEADS

    @pl.when(j == 0)
    def _():
        m_ref[...] = jnp.full(m_ref.shape, NEG_INF, F32)
        l_ref[...] = jnp.zeros(l_ref.shape, F32)
        acc_ref[...] = jnp.zeros(acc_ref.shape, F32)
        carry_ref[...] = jnp.zeros(carry_ref.shape, F32)

    q = q_ref[...].astype(F32)
    head_of_lane = _lane_iota((FOX_HEADS, D_FOX)) // HEAD_DIM
    own = head_of_lane == _row_iota((FOX_HEADS, D_FOX))
    wq = jnp.concatenate([jnp.where(own, jnp.broadcast_to(q[t:t + 1, :], (FOX_HEADS, D_FOX)), 0.0)
                          for t in range(n_new)], axis=0).astype(BF16)

    def update(s, mask, v):
        lg = s if mask is None else jnp.where(mask, s, NEG_INF)
        m_old = m_ref[...]
        m_new = jnp.maximum(m_old, jnp.max(lg, axis=-1, keepdims=True))
        p = jnp.exp(lg - m_new)
        if mask is not None:
            p = jnp.where(mask, p, 0.0)
        alpha = jnp.exp(m_old - m_new)
        l_ref[...] = alpha * l_ref[...] + jnp.sum(p, axis=-1, keepdims=True)
        m_ref[...] = m_new
        acc_ref[...] = alpha * acc_ref[...] + _dot_nt(p.astype(BF16), v)

    after = jnp.where(_row_iota((PAGE_SIZE, PAGE_SIZE)) > _lane_iota((PAGE_SIZE, PAGE_SIZE)), 1.0, 0.0).astype(BF16)
    carry = carry_ref[...]
    s_parts = []
    for u in range(pps):
        lft = lp[u][0]
        suffix = _dot3(lft, after) + carry[:, 0:1]
        carry = carry + jnp.sum(lft, axis=-1, keepdims=True)
        bias = jnp.concatenate([suffix] * n_new, axis=0)
        s_parts.append(_dot(wq, kp[u][0].astype(BF16)) + bias)
    carry_ref[...] = carry
    vcat = jnp.concatenate([r[0].astype(BF16) for r in vp], axis=1)
    update(jnp.concatenate(s_parts, axis=1), None, vcat)

    @pl.when(j == pl.num_programs(1) - 1)
    def _():
        upto = jnp.where(_row_iota((PAGE_SIZE, PAGE_SIZE)) <= _lane_iota((PAGE_SIZE, PAGE_SIZE)), 1.0, 0.0)
        c_new = _dot3(lnew_ref[0], upto.astype(BF16))
        s_new = _dot(wq, knew_ref[0].astype(BF16)) - jnp.concatenate([c_new] * n_new, axis=0)
        mask = _lane_iota((rows, PAGE_SIZE)) <= (_row_iota((rows, PAGE_SIZE)) // FOX_HEADS)
        update(s_new, mask, vnew_ref[0].astype(BF16))
        o = acc_ref[...] / jnp.maximum(l_ref[...], TINY)
        outs = [jnp.sum(jnp.where(own, o[t * FOX_HEADS:(t + 1) * FOX_HEADS], 0.0), axis=0, keepdims=True)
                for t in range(n_new)]
        o_ref[...] = jnp.concatenate(outs + [jnp.zeros((Q_PAD - n_new, D_FOX), F32)], axis=0)


def _fox_decode(qpad, pool_k, pool_v, pool_lft, knew, vnew, lnew_t, page_table, n_new, pps):
    nb, n_pages = page_table.shape
    assert n_pages % pps == 0
    steps = n_pages // pps

    def page_spec(u, shape):
        return pl.BlockSpec((1,) + shape, lambda b, j, pt: (pt[b, n_pages - 1 - (j * pps + u)], 0, 0))

    per_b = lambda shape: pl.BlockSpec((1,) + shape, lambda b, j, pt: (b, 0, 0))
    rows = n_new * FOX_HEADS
    return pl.pallas_call(
        functools.partial(_fox_decode_kernel, pps=pps, n_new=n_new),
        grid_spec=pltpu.PrefetchScalarGridSpec(
            num_scalar_prefetch=1,
            grid=(nb, steps),
            in_specs=[pl.BlockSpec((Q_PAD, D_FOX), lambda b, j, pt: (b, 0))]
            + [page_spec(u, (D_FOX, PAGE_SIZE)) for u in range(pps)] * 2
            + [page_spec(u, (FOX_HEADS, PAGE_SIZE)) for u in range(pps)]
            + [per_b((D_FOX, PAGE_SIZE)), per_b((D_FOX, PAGE_SIZE)), per_b((FOX_HEADS, PAGE_SIZE))],
            out_specs=pl.BlockSpec((Q_PAD, D_FOX), lambda b, j, pt: (b, 0)),
            scratch_shapes=[pltpu.VMEM((rows, 1), F32), pltpu.VMEM((rows, 1), F32), pltpu.VMEM((rows, D_FOX), F32),
                            pltpu.VMEM((FOX_HEADS, 1), F32)],
        ),
        out_shape=jax.ShapeDtypeStruct((nb * Q_PAD, D_FOX), F32),
        compiler_params=_params("parallel", "arbitrary"),
        name="fox_decode",
    )(page_table, qpad, *([pool_k] * pps), *([pool_v] * pps), *([pool_lft] * pps), knew, vnew, lnew_t)


def _pad_rows(a, rows):
    return jnp.pad(a, ((0, 0), (0, rows - a.shape[1]), (0, 0)))


def _sample_layer(x, cache, page_table, wts):
    (c_cmp_k, c_cmp_v, c_sel_k, c_sel_v, c_fox_k, c_fox_v, c_fox_logf, s_win_k, s_win_v, s_conv) = cache
    nb, t, _ = x.shape
    n = nb * t
    n_pool = c_cmp_k.shape[0]
    n_pages = page_table.shape[1]
    past = n_pages * PAGE_SIZE
    x2d = x.reshape(n, D_MODEL)
    pr = _project(x2d, wts, 1, n, n)
    per_seq = lambda a: a.reshape(nb, t, a.shape[-1])
    qpad = _pad_rows(per_seq(pr['qa']), Q_PAD).reshape(nb * Q_PAD, D_QA)
    take = lambda o: o.reshape(nb, Q_PAD, -1)[:, :t].reshape(n, -1)
    new_t = lambda a: jnp.swapaxes(a[0].reshape(a.shape[1], nb, t), 0, 1)
    lane_pad = lambda a, w: jnp.pad(a, ((0, 0), (0, 0), (0, w - a.shape[2])))
    paged = lambda pool: jnp.moveaxis(pool, 1, -1)

    nc = (past + t) // CMP_STRIDE
    assert nc * CMP_STRIDE <= past
    lhk, lhv = _cmp_lh_paged(c_cmp_k.reshape(n_pool, CHUNKS_PER_PAGE, CHUNK_LANES),
                             c_cmp_v.reshape(n_pool, CHUNKS_PER_PAGE, CHUNK_LANES),
                             page_table, wts['cmpk'], wts['cmpv'], min(8, n_pages))
    kc, vc = _cmp_finish(lhk, lhv, wts['cmpk'], wts['cmpv'], wts['m64'], wts['gkc'], nc)
    n_slc = -(-(past + t) // SEL_BLOCK)
    o_cmp, _, idx = _cmp_attend(qpad, kc, vc, nb, Q_PAD, Q_PAD, nc, n_slc, past)

    ix = idx[:, :, :t, :N_SEL].reshape(nb, NSA_KV_HEADS * t * N_SEL)
    qsel = jnp.swapaxes(pr['qa'].reshape(nb, t, NSA_GROUP, NSA_KV_HEADS, HEAD_DIM), 2, 3)
    qsel = jnp.pad(qsel, ((0, 0), (0, 0), (0, 0), (0, Q_PAD - NSA_GROUP), (0, 0)))
    grouped = lambda a: lane_pad(new_t(a), PAGE_SIZE).reshape(nb, NSA_KV_HEADS, HEAD_DIM, PAGE_SIZE)
    o_sel = _sel_sample(qsel, paged(c_sel_k), paged(c_sel_v), grouped(pr['skT']), grouped(pr['svT']),
                        page_table, ix, past)
    o_sel = jnp.swapaxes(o_sel[:, :, :, :NSA_GROUP], 2, 3).reshape(n, D_QA)

    w_buf = s_win_k.shape[1]
    wk_all = jnp.concatenate([paged(s_win_k).reshape(nb, D_KVA, w_buf), new_t(pr['wkT'])], axis=2)
    wv_all = jnp.concatenate([paged(s_win_v).reshape(nb, D_KVA, w_buf), new_t(pr['wvT'])], axis=2)
    o_win = _win_sample(qpad, lane_pad(wk_all, w_buf + LANES), lane_pad(wv_all, w_buf + LANES), past, w_buf)

    fqpad = _pad_rows(per_seq(pr['fq']), Q_PAD).reshape(nb * Q_PAD, D_FOX)
    o_fox = _fox_decode(fqpad, paged(c_fox_k).reshape(n_pool, D_FOX, PAGE_SIZE),
                        paged(c_fox_v).reshape(n_pool, D_FOX, PAGE_SIZE), paged(c_fox_logf),
                        lane_pad(new_t(pr['fkT']), PAGE_SIZE), lane_pad(new_t(pr['fvT']), PAGE_SIZE),
                        lane_pad(new_t(pr['logfT']), PAGE_SIZE), page_table, t, min(8, n_pages))

    h = _merge(x2d, take(o_cmp), o_sel, take(o_win), pr['gates'], take(o_fox), pr['ga'], pr['gb'], wts, n)
    zeros = jnp.zeros((nb, D_FF), F32)
    prev = s_conv.astype(F32)
    p1 = jnp.stack([prev[:, 1]] + [zeros] * (t - 1), axis=1).reshape(n, D_FF)
    p2 = jnp.stack([prev[:, 0], prev[:, 1]] + [zeros] * (t - 2), axis=1).reshape(n, D_FF)
    y, u = _ffn(h, p1, p2, wts, n, t, n)
    conv_rows = u.reshape(nb, t, D_FF)[:, t - (CONV_W - 1):]
    rows = lambda a, heads: a[0].T.reshape(nb, t, heads, HEAD_DIM)
    win = lambda a: jnp.moveaxis(a[:, :, t:t + w_buf].reshape(nb, NSA_KV_HEADS, HEAD_DIM, w_buf), -1, 1)
    state = (rows(pr['ckT'], NSA_KV_HEADS), rows(pr['cvT'], NSA_KV_HEADS), rows(pr['skT'], NSA_KV_HEADS),
             rows(pr['svT'], NSA_KV_HEADS), rows(pr['fkT'], FOX_HEADS), rows(pr['fvT'], FOX_HEADS),
             pr['logfT'][0].T.reshape(nb, t, FOX_HEADS), win(wk_all), win(wv_all), conv_rows)
    return y.reshape(nb, t, D_MODEL), state


def _tile_rows(n, pref):
    return pref if n % pref == 0 else n


def _prompt_layer(x, wts):
    nb, t, _ = x.shape
    n = nb * t
    x2d = x.reshape(n, D_MODEL)
    pr = _project(x2d, wts, nb, t, _tile_rows(t, 256))
    nc = t // CMP_STRIDE
    lhk, lhv = _cmp_lh_dense(pr['ck'].reshape(n // CMP_STRIDE, CHUNK_LANES), pr['cv'].reshape(n // CMP_STRIDE, CHUNK_LANES),
                             wts['cmpk'], wts['cmpv'], _tile_rows(n // CMP_STRIDE, 256))
    kc, vc = _cmp_finish(lhk, lhv, wts['cmpk'], wts['cmpv'], wts['m64'], wts['gkc'], nc)
    n_slc = -(-t // SEL_BLOCK)
    o_cmp, sel, _ = _cmp_attend(pr['qa'], kc, vc, nb, t, 128, nc, n_slc, 0)
    o_sel = _sel_prompt(pr['qa'], pr['skTb'], pr['svTb'], sel, nb, t, 128, min(512, t))
    o_win = _win_prompt(pr['qa'], pr['wkTb'], pr['wvTb'], nb, t, 128)
    bias = _fox_bias(pr['logfT'], nb, t)
    o_fox = _fox_prompt(pr['fq'], pr['fkTb'], pr['fvTb'], bias, nb, t, 256, min(512, t))
    h = _merge(x2d, o_cmp, o_sel, o_win, pr['gates'], o_fox, pr['ga'], pr['gb'], wts, _tile_rows(n, 256))
    tm = _tile_rows(t, 256)
    y, tail = _ffn(h, None, None, wts, tm, t, 8)
    conv_rows = tail.reshape(nb, t // tm, 8, D_FF)[:, -1, 8 - (CONV_W - 1):]
    rows = lambda a, heads: jnp.moveaxis(a.reshape(nb, heads, HEAD_DIM, a.shape[-1]), -1, 1)
    w_keep = min(WINDOW, t)
    state = (rows(pr['ckT'], NSA_KV_HEADS), rows(pr['cvT'], NSA_KV_HEADS), rows(pr['skT'], NSA_KV_HEADS),
             rows(pr['svT'], NSA_KV_HEADS), rows(pr['fkT'], FOX_HEADS), rows(pr['fvT'], FOX_HEADS),
             jnp.swapaxes(pr['logfT'], 1, 2), rows(pr['wkT'][:, :, t - w_keep:], NSA_KV_HEADS),
             rows(pr['wvT'][:, :, t - w_keep:], NSA_KV_HEADS), conv_rows)
    return y.reshape(nb, t, D_MODEL), state


_WEIGHT_NAMES = ('norm_attn_g', 'w_in', 'b_forget', 'q_norm_a_g', 'k_norm_cmp_g', 'k_norm_sel_g', 'k_norm_win_g',
                 'cmp_pe_k', 'cmp_w1_k', 'cmp_b1_k', 'cmp_w2_k', 'cmp_pe_v', 'cmp_w1_v', 'cmp_b1_v', 'cmp_w2_v',
                 'q_norm_b_g', 'k_norm_b_g', 'w_o_a', 'w_o_b', 'w_out', 'norm_ffn_g', 'w_up', 'w_gate', 'conv_w',
                 'conv_b', 'w_down')


def kernel(x_prompt, x_sample, cache_cmp_k, cache_cmp_v, cache_sel_k, cache_sel_v, cache_fox_k, cache_fox_v,
           cache_fox_logf, state_win_k, state_win_v, state_ffn_conv, page_table,
           norm_attn_g, w_in, b_forget, q_norm_a_g, k_norm_cmp_g, k_norm_sel_g, k_norm_win_g,
           cmp_pe_k, cmp_w1_k, cmp_b1_k, cmp_w2_k, cmp_pe_v, cmp_w1_v, cmp_b1_v, cmp_w2_v,
           q_norm_b_g, k_norm_b_g, w_o_a, w_o_b, w_out, norm_ffn_g, w_up, w_gate, conv_w, conv_b, w_down):
    weights = (norm_attn_g, w_in, b_forget, q_norm_a_g, k_norm_cmp_g, k_norm_sel_g, k_norm_win_g,
               cmp_pe_k, cmp_w1_k, cmp_b1_k, cmp_w2_k, cmp_pe_v, cmp_w1_v, cmp_b1_v, cmp_w2_v,
               q_norm_b_g, k_norm_b_g, w_o_a, w_o_b, w_out, norm_ffn_g, w_up, w_gate, conv_w, conv_b, w_down)
    caches = (cache_cmp_k, cache_cmp_v, cache_sel_k, cache_sel_v, cache_fox_k, cache_fox_v, cache_fox_logf,
              state_win_k, state_win_v, state_ffn_conv)
    depth = w_in.shape[0]
    prompt_states, sample_states = [], []
    for layer in range(depth):
        wts = _prep_weights({name: w[layer] for name, w in zip(_WEIGHT_NAMES, weights)})
        x_prompt, st_p = _prompt_layer(x_prompt, wts)
        x_sample, st_s = _sample_layer(x_sample, tuple(c[layer] for c in caches), page_table, wts)
        prompt_states.append(st_p)
        sample_states.append(st_s)
    new_prompt = [jnp.stack(rows) for rows in zip(*prompt_states)]
    new_sample = [jnp.stack(rows) for rows in zip(*sample_states)]
    return (x_prompt, x_sample, *new_prompt, *new_sample)
```

```python
import functools

import numpy as np
import jax
import jax.numpy as jnp
from jax import lax
from jax.experimental import pallas as pl
from jax.experimental.pallas import tpu as pltpu

F32 = jnp.float32
BF16 = jnp.bfloat16

D_MODEL = 1024
HEAD_DIM = 64
NSA_HEADS = 8
NSA_KV_HEADS = 2
NSA_GROUP = NSA_HEADS // NSA_KV_HEADS
FOX_HEADS = 8
CMP_BLOCK = 32
CMP_STRIDE = 16
CMP_HIDDEN = 2 * HEAD_DIM
SEL_BLOCK = 64
N_SEL = 16
WINDOW = 512
D_FF = 3 * D_MODEL
CONV_W = 3
PAGE_SIZE = 128
D_QA = NSA_HEADS * HEAD_DIM
D_KVA = NSA_KV_HEADS * HEAD_DIM
D_FOX = FOX_HEADS * HEAD_DIM
IN_SIZES = (D_QA, D_KVA, D_KVA, D_KVA, D_KVA, D_KVA, D_KVA, 3 * NSA_HEADS, D_FOX, D_FOX, D_FOX, FOX_HEADS,
            D_MODEL, D_MODEL)
ATTN_SCALE = HEAD_DIM ** -0.5
FORCED_SCORE = 1e6
NEG_INF = -1e30
TINY = 1e-30
EPS = 1e-6
LANES = 128
N_GATE = 3 * NSA_HEADS
VMEM_LIMIT = 56 * 1024 * 1024


def _dot(a, b):
    return jnp.dot(a, b, preferred_element_type=F32)


def _dot_nt(a, b):
    return lax.dot_general(a, b, (((1,), (1,)), ((), ())), preferred_element_type=F32)


def _params(*sem):
    return pltpu.CompilerParams(dimension_semantics=sem, vmem_limit_bytes=VMEM_LIMIT)


def _resident(shape):
    nd = len(shape)
    return pl.BlockSpec(shape, lambda *_: (0,) * nd, pipeline_mode=pl.Buffered(1))


def _lane_iota(shape):
    return lax.broadcasted_iota(jnp.int32, shape, len(shape) - 1)


def _row_iota(shape):
    return lax.broadcasted_iota(jnp.int32, shape, len(shape) - 2)


def _gelu(x):
    return 0.5 * x * (1.0 + jnp.tanh(0.7978845608028654 * (x + 0.044715 * (x * x * x))))


def _sigmoid(x):
    return 1.0 / (1.0 + jnp.exp(-x))


def _log_sigmoid(x):
    return -(jnp.maximum(-x, 0.0) + jnp.log1p(jnp.exp(-jnp.abs(x))))


def _softmax_rows(logits, mask):
    lg = jnp.where(mask, logits, NEG_INF)
    m = jnp.max(lg, axis=-1, keepdims=True)
    e = jnp.where(mask, jnp.exp(lg - m), 0.0)
    return e / jnp.maximum(jnp.sum(e, axis=-1, keepdims=True), TINY)


_C_QA = 0
_C_CK = _C_QA + D_QA
_C_CV = _C_CK + D_KVA
_C_FQ = _C_CV + D_KVA
_C_GA = _C_FQ + D_FOX
_C_GB = _C_GA + D_MODEL
_C_MISC = _C_GB + D_MODEL
_C_END = _C_MISC + LANES
_R_KV = 0
_R_FK = _R_KV + 6 * D_KVA
_R_FV = _R_FK + D_FOX
_R_FF = _R_FV + D_FOX
_R_END = _R_FF + 16


def _qa_perm():
    cols = []
    for p in range(NSA_GROUP):
        cols += list(range(p * HEAD_DIM, (p + 1) * HEAD_DIM))
        cols += list(range((NSA_GROUP + p) * HEAD_DIM, (NSA_GROUP + p + 1) * HEAD_DIM))
    return np.asarray(cols, np.int32)


def _permute_w_in(w_in):
    offs = [int(o) for o in np.concatenate([[0], np.cumsum(IN_SIZES)])]
    o_ck, o_sk, o_ng, o_fq, o_fk, o_ff, o_ga = offs[1], offs[3], offs[7], offs[8], offs[9], offs[11], offs[12]
    cols = np.concatenate([_qa_perm(), np.arange(o_ck, o_sk), np.arange(o_fq, o_fk), np.arange(o_ga, offs[-1]),
                           np.arange(o_ng, o_fq)])
    pad = jnp.zeros((w_in.shape[0], LANES - N_GATE), w_in.dtype)
    w_tok = jnp.concatenate([w_in[:, cols], pad], axis=1).astype(BF16)
    rows = np.concatenate([np.arange(o_ck, o_ng), np.arange(o_fk, o_ff), np.arange(o_ff, o_ga)])
    w_t = w_in.T
    w_feat = jnp.concatenate([w_t[rows], jnp.zeros((_R_END - len(rows), w_in.shape[0]), w_in.dtype)], axis=0)
    return w_tok, w_feat.astype(BF16)


def _head_mean_matrix():
    i = np.arange(D_QA)
    return jnp.asarray((i[:, None] // HEAD_DIM == i[None, :] // HEAD_DIM) / HEAD_DIM, BF16)


_PROJ_TOKEN_OUTS = (('qa', D_QA, BF16), ('ck', D_KVA, F32), ('cv', D_KVA, F32), ('fq', D_FOX, BF16),
                    ('ga', D_MODEL, F32), ('gb', D_MODEL, F32), ('gates', LANES, F32))
_PROJ_FEATURE_OUTS = (('ckT', D_KVA, F32), ('cvT', D_KVA, F32), ('skT', D_KVA, F32), ('svT', D_KVA, F32),
                      ('wkT', D_KVA, F32), ('wvT', D_KVA, F32), ('fkT', D_FOX, F32), ('fvT', D_FOX, F32),
                      ('logfT', FOX_HEADS, F32), ('skTb', D_KVA, BF16), ('svTb', D_KVA, BF16), ('wkTb', D_KVA, BF16),
                      ('wvTb', D_KVA, BF16), ('fkTb', D_FOX, BF16), ('fvTb', D_FOX, BF16))


def _proj_kernel(x_ref, gattn_ref, w_ref, wt_ref, m64_ref, gq_ref, gfq_ref, gsk_ref, gwk_ref, gfk_ref, bf_ref,
                 qa_ref, ck_ref, cv_ref, fq_ref, ga_ref, gb_ref, gates_ref,
                 ckT_ref, cvT_ref, skT_ref, svT_ref, wkT_ref, wvT_ref, fkT_ref, fvT_ref, logfT_ref,
                 skTb_ref, svTb_ref, wkTb_ref, wvTb_ref, fkTb_ref, fvTb_ref):
    x = x_ref[...]
    ms = jnp.mean(x * x, axis=-1, keepdims=True)
    xn = (x * lax.rsqrt(ms + EPS) * gattn_ref[...]).astype(BF16)

    def seg(a, width):
        return _dot(xn, w_ref[:, a:a + width])

    def seg_t(a, width):
        return _dot_nt(wt_ref[a:a + width, :], xn)

    def head_norm(z, g_ref):
        w = z.shape[-1]
        hm = _dot((z * z).astype(BF16), m64_ref[0:w, 0:w])
        return z * lax.rsqrt(hm + EPS) * g_ref[...]

    def store_t(z, f32_ref, bf_ref_, g_ref):
        for h in range(z.shape[0] // HEAD_DIM):
            rows = slice(h * HEAD_DIM, (h + 1) * HEAD_DIM)
            zh = z[rows]
            if g_ref is not None:
                zh = zh * lax.rsqrt(jnp.mean(zh * zh, axis=0, keepdims=True) + EPS) * g_ref[rows, :]
            f32_ref[0, rows, :] = zh
            if bf_ref_ is not None:
                bf_ref_[0, rows, :] = zh.astype(BF16)

    qa_ref[...] = head_norm(seg(_C_QA, D_QA), gq_ref).astype(BF16)
    ck_ref[...] = seg(_C_CK, D_KVA)
    cv_ref[...] = seg(_C_CV, D_KVA)
    fq_ref[...] = head_norm(seg(_C_FQ, D_FOX), gfq_ref).astype(BF16)
    ga_ref[...] = _sigmoid(seg(_C_GA, D_MODEL))
    gb_ref[...] = _sigmoid(seg(_C_GB, D_MODEL))
    gates_ref[...] = _sigmoid(seg(_C_MISC, LANES))

    store_t(seg_t(_R_KV, D_KVA), ckT_ref, None, None)
    store_t(seg_t(_R_KV + D_KVA, D_KVA), cvT_ref, None, None)
    store_t(seg_t(_R_KV + 2 * D_KVA, D_KVA), skT_ref, skTb_ref, gsk_ref)
    store_t(seg_t(_R_KV + 3 * D_KVA, D_KVA), svT_ref, svTb_ref, None)
    store_t(seg_t(_R_KV + 4 * D_KVA, D_KVA), wkT_ref, wkTb_ref, gwk_ref)
    store_t(seg_t(_R_KV + 5 * D_KVA, D_KVA), wvT_ref, wvTb_ref, None)
    store_t(seg_t(_R_FK, D_FOX), fkT_ref, fkTb_ref, gfk_ref)
    store_t(seg_t(_R_FV, D_FOX), fvT_ref, fvTb_ref, None)
    logfT_ref[0] = _log_sigmoid(seg_t(_R_FF, _R_END - _R_FF)[0:FOX_HEADS] + bf_ref[...])


def _project(x2d, wts, nb, t, tm):
    assert t % tm == 0
    nt = t // tm
    row = lambda w: pl.BlockSpec((tm, w), lambda i: (i, 0))
    feat = lambda w: pl.BlockSpec((1, w, tm), lambda i: (i // nt, 0, i % nt))
    small = [wts['g_attn'], wts['w_tok'], wts['w_feat'], wts['m64'], wts['gq'], wts['gfq'], wts['gsk_col'],
             wts['gwk_col'], wts['gfk_col'], wts['bf_col']]
    res = pl.pallas_call(
        _proj_kernel,
        grid=(nb * nt,),
        in_specs=[row(D_MODEL)] + [_resident(a.shape) for a in small],
        out_specs=[row(w) for _, w, _ in _PROJ_TOKEN_OUTS] + [feat(w) for _, w, _ in _PROJ_FEATURE_OUTS],
        out_shape=[jax.ShapeDtypeStruct((nb * t, w), dt) for _, w, dt in _PROJ_TOKEN_OUTS]
        + [jax.ShapeDtypeStruct((nb, w, t), dt) for _, w, dt in _PROJ_FEATURE_OUTS],
        compiler_params=_params("parallel"),
        name="proj",
    )(x2d, *small)
    return dict(zip([n for n, _, _ in _PROJ_TOKEN_OUTS + _PROJ_FEATURE_OUTS], res))


CHUNK_LANES = CMP_STRIDE * D_KVA
CHUNKS_PER_PAGE = PAGE_SIZE // CMP_STRIDE


def _cmp_weights(pe, w1, b1, w2):
    eye = jnp.eye(NSA_KV_HEADS, dtype=w1.dtype)
    big = lambda w: jnp.einsum('pdh,gk->pgdkh', w, eye).reshape(CHUNK_LANES, NSA_KV_HEADS * CMP_HIDDEN).astype(BF16)
    flat = lambda e: jnp.broadcast_to(e[:, None, :], (CMP_STRIDE, NSA_KV_HEADS, HEAD_DIM)).reshape(1, CHUNK_LANES)
    w2bd = jnp.einsum('he,gk->ghke', w2, eye).reshape(NSA_KV_HEADS * CMP_HIDDEN, D_KVA).astype(BF16)
    return dict(w1lo=big(w1[:CMP_STRIDE]), w1hi=big(w1[CMP_STRIDE:]), pelo=flat(pe[:CMP_STRIDE]),
                pehi=flat(pe[CMP_STRIDE:]), b1=jnp.tile(b1, NSA_KV_HEADS)[None, :], w2=w2bd)


def _cmp_lh_body(x, pelo, pehi, w1lo, w1hi):
    lo = _dot((x + pelo).astype(BF16), w1lo)
    hi = _dot((x + pehi).astype(BF16), w1hi)
    return jnp.concatenate([lo, hi], axis=1)


def _cmp_lh_kernel(*refs, n_blocks, paged):
    xk = refs[:n_blocks]
    xv = refs[n_blocks:2 * n_blocks]
    pk = refs[2 * n_blocks:2 * n_blocks + 4]
    pv = refs[2 * n_blocks + 4:2 * n_blocks + 8]
    ok_ref, ov_ref = refs[2 * n_blocks + 8:]

    def rows(blocks):
        if not paged:
            return blocks[0][...]
        return jnp.concatenate(
            [jnp.concatenate([b[0, pl.ds(p, CHUNKS_PER_PAGE, stride=CMP_STRIDE), :] for p in range(CMP_STRIDE)],
                             axis=1) for b in blocks], axis=0)

    ok_ref[...] = _cmp_lh_body(rows(xk), pk[0][...], pk[1][...], pk[2][...], pk[3][...])
    ov_ref[...] = _cmp_lh_body(rows(xv), pv[0][...], pv[1][...], pv[2][...], pv[3][...])


def _cmp_lh_dense(xk, xv, cwk, cwv, tm):
    n = xk.shape[0]
    assert n % tm == 0
    row = lambda w: pl.BlockSpec((tm, w), lambda i: (i, 0))
    par = [cwk['pelo'], cwk['pehi'], cwk['w1lo'], cwk['w1hi'], cwv['pelo'], cwv['pehi'], cwv['w1lo'], cwv['w1hi']]
    wlh = 2 * NSA_KV_HEADS * CMP_HIDDEN
    return pl.pallas_call(
        functools.partial(_cmp_lh_kernel, n_blocks=1, paged=False),
        grid=(n // tm,),
        in_specs=[row(CHUNK_LANES), row(CHUNK_LANES)] + [_resident(a.shape) for a in par],
        out_specs=[row(wlh), row(wlh)],
        out_shape=[jax.ShapeDtypeStruct((n, wlh), F32)] * 2,
        compiler_params=_params("parallel"),
        name="cmp_lh",
    )(xk, xv, *par)


def _cmp_lh_paged(pool_k, pool_v, page_table, cwk, cwv, pages_per_step):
    nb, n_pages = page_table.shape
    assert n_pages % pages_per_step == 0
    steps = n_pages // pages_per_step

    def page_spec(u):
        return pl.BlockSpec((1, PAGE_SIZE, D_KVA), lambda b, j, pt: (pt[b, j * pages_per_step + u], 0, 0))

    par = [cwk['pelo'], cwk['pehi'], cwk['w1lo'], cwk['w1hi'], cwv['pelo'], cwv['pehi'], cwv['w1lo'], cwv['w1hi']]
    wlh = 2 * NSA_KV_HEADS * CMP_HIDDEN
    rows = pages_per_step * CHUNKS_PER_PAGE
    out_spec = pl.BlockSpec((rows, wlh), lambda b, j, pt: (b * steps + j, 0))
    const = lambda a: pl.BlockSpec(a.shape, lambda b, j, pt: (0,) * a.ndim, pipeline_mode=pl.Buffered(1))
    def kern(pt_ref, *refs):
        _cmp_lh_kernel(*refs, n_blocks=pages_per_step, paged=True)

    return pl.pallas_call(
        kern,
        grid_spec=pltpu.PrefetchScalarGridSpec(
            num_scalar_prefetch=1,
            grid=(nb, steps),
            in_specs=[page_spec(u) for u in range(pages_per_step)] * 2 + [const(a) for a in par],
            out_specs=[out_spec, out_spec],
        ),
        out_shape=[jax.ShapeDtypeStruct((nb * n_pages * CHUNKS_PER_PAGE, wlh), F32)] * 2,
        compiler_params=_params("parallel", "parallel"),
        name="cmp_lh_paged",
    )(page_table, *([pool_k] * pages_per_step), *([pool_v] * pages_per_step), *par)


def _cmp_finish_kernel(lhk_ref, lhv_ref, b1k_ref, w2k_ref, b1v_ref, w2v_ref, m64_ref, gk_ref, kc_ref, vc_ref):
    def finish(lh_ref, b1_ref, w2_ref):
        lh = lh_ref[...]
        nc = lh.shape[0]
        half = lh.shape[1] // 2
        hi_next = pltpu.roll(lh[:, half:], nc - 1, 0)
        h = _gelu(lh[:, :half] + hi_next + b1_ref[...])
        return _dot(h.astype(BF16), w2_ref[...])

    kc = finish(lhk_ref, b1k_ref, w2k_ref)
    hm = _dot((kc * kc).astype(BF16), m64_ref[0:D_KVA, 0:D_KVA])
    kc_ref[...] = (kc * lax.rsqrt(hm + EPS) * gk_ref[...]).astype(BF16)
    vc_ref[...] = finish(lhv_ref, b1v_ref, w2v_ref).astype(BF16)


def _cmp_finish(lhk, lhv, cwk, cwv, m64, gk, nc):
    n = lhk.shape[0]
    nb = n // nc
    seq = lambda w: pl.BlockSpec((nc, w), lambda b: (b, 0))
    par = [cwk['b1'], cwk['w2'], cwv['b1'], cwv['w2'], m64, gk]
    return pl.pallas_call(
        _cmp_finish_kernel,
        grid=(nb,),
        in_specs=[seq(lhk.shape[1]), seq(lhv.shape[1])] + [_resident(a.shape) for a in par],
        out_specs=[seq(D_KVA), seq(D_KVA)],
        out_shape=[jax.ShapeDtypeStruct((n, D_KVA), BF16)] * 2,
        compiler_params=_params("parallel"),
        name="cmp_finish",
    )(lhk, lhv, *par)


def _alibi_slope(g, r):
    return 2.0 ** -(NSA_GROUP * g + r + 1)


def _stack_heads(q):
    return jnp.concatenate([q[:, p * LANES:(p + 1) * LANES] for p in range(NSA_GROUP)], axis=0)


def _group_lanes(x, g):
    return jnp.where((_lane_iota(x.shape) // HEAD_DIM) == g, x, jnp.zeros_like(x))


def _unstack_heads(o0, o1, tq):
    lo = _lane_iota((tq, LANES)) < HEAD_DIM
    return jnp.concatenate([jnp.where(lo, o0[p * tq:(p + 1) * tq], o1[p * tq:(p + 1) * tq])
                            for p in range(NSA_GROUP)], axis=1)


MASK_BIAS = -1e9


def _cmp_group(qs, kc, vc, g, distf, mask, tq):
    s = _dot_nt(qs, _group_lanes(kc, g))
    ps = [_softmax_rows(s[r * tq:(r + 1) * tq] - _alibi_slope(g, r) * distf, mask) for r in range(NSA_GROUP)]
    out = _dot(jnp.concatenate(ps, axis=0).astype(BF16), vc)
    return out, (ps[0] + ps[1]) + (ps[2] + ps[3])


def _split2(x):
    hi = x.astype(BF16)
    return hi, (x - hi.astype(F32)).astype(BF16)


def _block_scores(p_slc, blk, qp, n_slc):
    cur = lax.shift_right_logical(qp, 6)
    forced = (blk == 0) | (blk == cur) | (blk == cur - 1)
    score = jnp.where(forced, FORCED_SCORE, jnp.where(blk * SEL_BLOCK <= qp, p_slc, -1.0))
    return jnp.where(blk < n_slc, score, -2.0), cur


def _cmp_dist(qpos0, tq, nck):
    dist = (qpos0 + _row_iota((tq, nck))) - (_lane_iota((tq, nck)) * CMP_STRIDE + (CMP_BLOCK - 1))
    return dist.astype(F32), dist >= 0


def _cmp_attn_idx_kernel(q_ref, kc_ref, vc_ref, o_ref, idx_ref, *, tq, n_slc, pos0):
    nck = kc_ref.shape[0]
    nsl = -(-n_slc // LANES) * LANES
    qpos0 = pos0 + pl.program_id(1) * tq
    qs = _stack_heads(q_ref[...])
    distf, mask = _cmp_dist(qpos0, tq, nck)
    c_start = _row_iota((nck, nsl)) * CMP_STRIDE
    s_start = _lane_iota((nck, nsl)) * SEL_BLOCK
    overlap = jnp.where((c_start < s_start + SEL_BLOCK) & (c_start + CMP_BLOCK > s_start), 1.0, 0.0).astype(BF16)
    blk = _lane_iota((tq, nsl))
    blkf = blk.astype(F32)
    slot = _lane_iota((tq, LANES))
    outs = []
    for g in range(NSA_KV_HEADS):
        out, psum = _cmp_group(qs, kc_ref[...], vc_ref[...], g, distf, mask, tq)
        outs.append(out)
        hi, lo = _split2(psum)
        score, _ = _block_scores(_dot(hi, overlap) + _dot(lo, overlap), blk, qpos0 + _row_iota((tq, nsl)), n_slc)
        idx = jnp.zeros((tq, LANES), F32)
        for it in range(N_SEL):
            m = jnp.max(score, axis=-1, keepdims=True)
            j = jnp.min(jnp.where(score == m, blkf, 1e9), axis=-1, keepdims=True)
            score = jnp.where(blkf == j, -3.0, score)
            idx = jnp.where(slot == it, j, idx)
        idx_ref[0, g] = idx.astype(jnp.int32)
    o_ref[...] = _unstack_heads(outs[0], outs[1], tq)


def _cmp_attn_feat_kernel(q_ref, kc_ref, vc_ref, o_ref, qaux_ref, *, tq, n_slc):
    nck = kc_ref.shape[0]
    rows = -(-n_slc // 8) * 8
    qpos0 = pl.program_id(1) * tq
    qs = _stack_heads(q_ref[...])
    distf, mask = _cmp_dist(qpos0, tq, nck)
    s_start = _row_iota((LANES, nck)) * SEL_BLOCK
    c_start = _lane_iota((LANES, nck)) * CMP_STRIDE
    overlap_t = jnp.where((c_start < s_start + SEL_BLOCK) & (c_start + CMP_BLOCK > s_start), 1.0, 0.0).astype(BF16)
    blk = _row_iota((rows, tq))
    qp = qpos0 + _lane_iota((rows, tq))
    outs = []
    for g in range(NSA_KV_HEADS):
        out, psum = _cmp_group(qs, kc_ref[...], vc_ref[...], g, distf, mask, tq)
        outs.append(out)
        hi, lo = _split2(psum)
        p_slc_t = _dot_nt(overlap_t, hi) + _dot_nt(overlap_t, lo)
        score, cur = _block_scores(p_slc_t[0:rows], blk, qp, n_slc)
        rank = jnp.zeros((rows, tq), F32)
        for i in range(n_slc):
            si = score[i:i + 1, :]
            rank = rank + jnp.where((si > score) | ((si == score) & (blk > i)), 1.0, 0.0)
        feat = jnp.where((rank < N_SEL) & (blk <= cur), 0.0, MASK_BIAS)
        feat = jnp.concatenate([feat, jnp.zeros((LANES - rows, tq), F32)], axis=0)
        qaux_ref[0, g] = feat.T.astype(BF16)
    o_ref[...] = _unstack_heads(outs[0], outs[1], tq)


def _cmp_attend(q2d, kc2d, vc2d, nb, t, tq, nck, n_slc, pos0, want):
    nt = t // tq
    if want == 'idx':
        kern = functools.partial(_cmp_attn_idx_kernel, tq=tq, n_slc=n_slc, pos0=pos0)
        second = jnp.int32
    else:
        assert pos0 == 0 and n_slc < AUX_BLOCK_HI
        kern = functools.partial(_cmp_attn_feat_kernel, tq=tq, n_slc=n_slc)
        second = BF16
    return pl.pallas_call(
        kern,
        grid=(nb, nt),
        in_specs=[pl.BlockSpec((tq, D_QA), lambda b, i: (b * nt + i, 0)),
                  pl.BlockSpec((nck, D_KVA), lambda b, i: (b, 0)),
                  pl.BlockSpec((nck, D_KVA), lambda b, i: (b, 0))],
        out_specs=[pl.BlockSpec((tq, D_QA), lambda b, i: (b * nt + i, 0)),
                   pl.BlockSpec((1, NSA_KV_HEADS, tq, LANES), lambda b, i: (b, 0, i, 0))],
        out_shape=[jax.ShapeDtypeStruct((nb * t, D_QA), F32),
                   jax.ShapeDtypeStruct((nb, NSA_KV_HEADS, t, LANES), second)],
        compiler_params=_params("parallel", "parallel"),
        name="cmp_attn",
    )(q2d, kc2d, vc2d)


FLASH_ROWS = 16
FLASH_UNROLL_MAX = 64
FLASH_UNROLL_EXP = 64


def _flash_scratch(rows, tk):
    return [pltpu.VMEM((rows, LANES), F32), pltpu.VMEM((rows, LANES), F32), pltpu.VMEM((rows, 2 * LANES), F32),
            pltpu.VMEM((rows, tk), F32), pltpu.VMEM((rows, tk), BF16)]


def _flash_init(m_ref, acc_ref):
    m_ref[...] = jnp.full(m_ref.shape, NEG_INF, F32)
    acc_ref[...] = jnp.zeros(acc_ref.shape, F32)


def _flash_update(s_ref, p_ref, mask_fn, v_t, m_ref, alpha_ref, acc_ref):
    rows, tk = s_ref.shape

    def masked(r0, sl):
        s = s_ref[sl, :]
        return s if mask_fn is None else jnp.where(mask_fn(r0), s, NEG_INF)

    def row_max(c, carry):
        r0 = pl.multiple_of(c * FLASH_ROWS, FLASH_ROWS)
        sl = pl.ds(r0, FLASH_ROWS)
        m_old = m_ref[sl, :]
        m_new = jnp.maximum(m_old, jnp.max(masked(r0, sl), axis=-1, keepdims=True))
        alpha_ref[sl, :] = jnp.exp(m_old - m_new)
        m_ref[sl, :] = m_new
        return carry

    def exponentiate(c, carry):
        r0 = pl.multiple_of(c * FLASH_ROWS, FLASH_ROWS)
        sl = pl.ds(r0, FLASH_ROWS)
        s = masked(r0, sl)
        m = m_ref[sl, :]
        for j in range(tk // LANES):
            cols = slice(j * LANES, (j + 1) * LANES)
            p_ref[sl, cols] = jnp.exp(s[:, cols] - m).astype(BF16)
        return carry

    lax.fori_loop(0, rows // FLASH_ROWS, row_max, 0, unroll=FLASH_UNROLL_MAX)
    lax.fori_loop(0, rows // FLASH_ROWS, exponentiate, 0, unroll=FLASH_UNROLL_EXP)
    v_aug = jnp.concatenate([v_t, jnp.ones_like(v_t)], axis=0)
    alpha = alpha_ref[...]
    acc_ref[...] = jnp.concatenate([alpha, alpha], axis=1) * acc_ref[...] + _dot_nt(p_ref[...], v_aug)


def _flash_result(acc_ref):
    acc = acc_ref[...]
    return acc[:, :LANES] / jnp.maximum(acc[:, LANES:], TINY)


def _causal_chunk_mask(qpos0, k0, tq, tk):
    assert tq & (tq - 1) == 0 and tq % FLASH_ROWS == 0

    def mask_fn(r0):
        qpos = qpos0 + (r0 & (tq - 1)) + _row_iota((FLASH_ROWS, tk))
        return (k0 + _lane_iota((FLASH_ROWS, tk))) <= qpos

    return mask_fn


AUX_BLOCK_HI = 120
AUX_BLOCK_LO = 121


def _sel_key_features(t):
    assert t // SEL_BLOCK <= AUX_BLOCK_HI
    s = np.arange(t)
    tab = np.zeros((LANES, t), np.float32)
    tab[s // SEL_BLOCK, s] = 1.0
    tab[AUX_BLOCK_HI] = s // SEL_BLOCK
    tab[AUX_BLOCK_LO] = s % SEL_BLOCK
    return jnp.asarray(tab, BF16)


def _sel_prompt_kernel(q_ref, k_ref, v_ref, kaux_ref, qaux_ref, o_ref, m_ref, alpha_ref, acc_ref, s_ref, p_ref, *,
                       tq, tk):
    i = pl.program_id(1)
    qpos0 = i * tq
    kt_diag = qpos0 // tk
    qs = _stack_heads(q_ref[...])
    lane = _lane_iota((tq, LANES))
    causal = _causal_chunk_mask(qpos0, kt_diag * tk, tq, tk)
    outs = []
    for g in range(NSA_KV_HEADS):
        feat = qaux_ref[0, g]
        aux = jnp.concatenate(
            [jnp.where(lane == AUX_BLOCK_HI, _alibi_slope(g, r) * SEL_BLOCK,
                       jnp.where(lane == AUX_BLOCK_LO, _alibi_slope(g, r), feat)) for r in range(NSA_GROUP)], axis=0)
        q_aug = jnp.concatenate([_group_lanes(qs, g), aux], axis=1)
        _flash_init(m_ref, acc_ref)

        def tile(kt, mask):
            k0 = pl.multiple_of(kt * tk, tk)
            k_aug = jnp.concatenate([k_ref[0, :, pl.ds(k0, tk)], kaux_ref[:, pl.ds(k0, tk)]], axis=0)
            s_ref[...] = _dot(q_aug, k_aug)
            _flash_update(s_ref, p_ref, mask, v_ref[0, :, pl.ds(k0, tk)], m_ref, alpha_ref, acc_ref)

        def body(kt, carry):
            tile(kt, None)
            return carry

        lax.fori_loop(0, kt_diag, body, 0)
        tile(kt_diag, causal)
        outs.append(_flash_result(acc_ref))
    o_ref[...] = _unstack_heads(outs[0], outs[1], tq)


def _sel_prompt(q2d, k_t, v_t, qaux, nb, t, tq, tk):
    assert tk % tq == 0 and t % tk == 0
    nt = t // tq
    return pl.pallas_call(
        functools.partial(_sel_prompt_kernel, tq=tq, tk=tk),
        grid=(nb, nt),
        in_specs=[pl.BlockSpec((tq, D_QA), lambda b, i: (b * nt + i, 0)),
                  pl.BlockSpec((1, D_KVA, t), lambda b, i: (b, 0, 0)),
                  pl.BlockSpec((1, D_KVA, t), lambda b, i: (b, 0, 0)),
                  _resident((LANES, t)),
                  pl.BlockSpec((1, NSA_KV_HEADS, tq, LANES), lambda b, i: (b, 0, i, 0))],
        out_specs=pl.BlockSpec((tq, D_QA), lambda b, i: (b * nt + i, 0)),
        out_shape=jax.ShapeDtypeStruct((nb * t, D_QA), F32),
        scratch_shapes=_flash_scratch(NSA_GROUP * tq, tk),
        compiler_params=_params("parallel", "parallel"),
        name="sel_prompt",
    )(q2d, k_t, v_t, _sel_key_features(t), qaux)


def _window_core(q, k_t, v_t, qpos0, kpos0, tq):
    nk = k_t.shape[1]
    qs = _stack_heads(q)
    kpos = kpos0 + _lane_iota((tq, nk))
    dist = (qpos0 + _row_iota((tq, nk))) - kpos
    mask = (dist >= 0) & (dist <= WINDOW) & (kpos >= 0)
    distf = dist.astype(F32)
    outs = []
    for g in range(NSA_KV_HEADS):
        s = _dot(_group_lanes(qs, g), k_t)
        ps = [_softmax_rows(s[r * tq:(r + 1) * tq] - _alibi_slope(g, r) * distf, mask).astype(BF16)
              for r in range(NSA_GROUP)]
        outs.append(_dot_nt(jnp.concatenate(ps, axis=0), v_t))
    return _unstack_heads(outs[0], outs[1], tq)


def _win_prompt_kernel(q_ref, k_ref, v_ref, o_ref, *, tq, span):
    i = pl.program_id(1)
    k0 = pl.multiple_of(jnp.maximum(i * tq + tq - span, 0), tq)
    o_ref[...] = _window_core(q_ref[...], k_ref[0, :, pl.ds(k0, span)], v_ref[0, :, pl.ds(k0, span)], i * tq, k0, tq)


def _win_prompt(q2d, k_t, v_t, nb, t, tq):
    nt = t // tq
    span = min(tq + WINDOW, t)
    return pl.pallas_call(
        functools.partial(_win_prompt_kernel, tq=tq, span=span),
        grid=(nb, nt),
        in_specs=[pl.BlockSpec((tq, D_QA), lambda b, i: (b * nt + i, 0)),
                  pl.BlockSpec((1, D_KVA, t), lambda b, i: (b, 0, 0)),
                  pl.BlockSpec((1, D_KVA, t), lambda b, i: (b, 0, 0))],
        out_specs=pl.BlockSpec((tq, D_QA), lambda b, i: (b * nt + i, 0)),
        out_shape=jax.ShapeDtypeStruct((nb * t, D_QA), F32),
        compiler_params=_params("parallel", "parallel"),
        name="win_prompt",
    )(q2d, k_t, v_t)


def _split3(x):
    a = x.astype(BF16)
    r = x - a.astype(F32)
    b = r.astype(BF16)
    c = (r - b.astype(F32)).astype(BF16)
    return a, b, c


def _dot3(x, m):
    a, b, c = _split3(x)
    return _dot(a, m) + (_dot(b, m) + _dot(c, m))


def _fox_bias_kernel(lf_ref, o_ref, *, t):
    upto = jnp.where(_row_iota((LANES, LANES)) <= _lane_iota((LANES, LANES)), 1.0, 0.0).astype(BF16)
    carry = jnp.zeros((FOX_HEADS, 1), F32)
    pad = jnp.zeros((LANES - 3 * FOX_HEADS, LANES), F32)
    for j in range(t // LANES):
        cols = slice(j * LANES, (j + 1) * LANES)
        c = _dot3(lf_ref[0, :, cols], upto) + carry
        carry = c[:, LANES - 1:LANES]
        terms = [term.astype(F32) for term in _split3(-c)]
        o_ref[0, :, cols] = jnp.concatenate(terms + [pad], axis=0).astype(BF16)


def _fox_bias(logf_t, nb, t):
    return pl.pallas_call(
        functools.partial(_fox_bias_kernel, t=t),
        grid=(nb,),
        in_specs=[pl.BlockSpec((1, FOX_HEADS, t), lambda b: (b, 0, 0))],
        out_specs=pl.BlockSpec((1, LANES, t), lambda b: (b, 0, 0)),
        out_shape=jax.ShapeDtypeStruct((nb, LANES, t), BF16),
        compiler_params=_params("parallel"),
        name="fox_bias",
    )(logf_t)


def _fox_prompt_kernel(q_ref, k_ref, v_ref, b_ref, o_ref, m_ref, alpha_ref, acc_ref, s_ref, p_ref, *, tq):
    i = pl.program_id(1)
    lane = _lane_iota((tq, LANES))
    lo_lanes = lane < HEAD_DIM
    causal = _causal_chunk_mask(0, 0, tq, tq)
    for p in range(FOX_HEADS // 2):
        qp = q_ref[:, p * LANES:(p + 1) * LANES]
        zero = jnp.zeros_like(qp)
        ones = lambda h: jnp.where((lane < 3 * FOX_HEADS) & (lane % FOX_HEADS == h), 1.0, 0.0).astype(BF16)
        q2 = jnp.concatenate([jnp.concatenate([jnp.where(lo_lanes, qp, zero), ones(2 * p)], axis=1),
                              jnp.concatenate([jnp.where(lo_lanes, zero, qp), ones(2 * p + 1)], axis=1)], axis=0)
        _flash_init(m_ref, acc_ref)

        def tile(kt, mask):
            k0 = pl.multiple_of(kt * tq, tq)
            k_aug = jnp.concatenate([k_ref[0, p * LANES:(p + 1) * LANES, pl.ds(k0, tq)], b_ref[0, :, pl.ds(k0, tq)]],
                                    axis=0)
            s_ref[...] = _dot(q2, k_aug)
            _flash_update(s_ref, p_ref, mask, v_ref[0, p * LANES:(p + 1) * LANES, pl.ds(k0, tq)],
                          m_ref, alpha_ref, acc_ref)

        def body(kt, carry):
            tile(kt, None)
            return carry

        lax.fori_loop(0, i, body, 0)
        tile(i, causal)
        o = _flash_result(acc_ref)
        o_ref[:, p * LANES:(p + 1) * LANES] = jnp.where(lo_lanes, o[:tq], o[tq:])


def _fox_prompt(q2d, k_t, v_t, bias, nb, t, tq):
    nt = t // tq
    return pl.pallas_call(
        functools.partial(_fox_prompt_kernel, tq=tq),
        grid=(nb, nt),
        in_specs=[pl.BlockSpec((tq, D_FOX), lambda b, i: (b * nt + i, 0)),
                  pl.BlockSpec((1, D_FOX, t), lambda b, i: (b, 0, 0)),
                  pl.BlockSpec((1, D_FOX, t), lambda b, i: (b, 0, 0)),
                  pl.BlockSpec((1, LANES, t), lambda b, i: (b, 0, 0))],
        out_specs=pl.BlockSpec((tq, D_FOX), lambda b, i: (b * nt + i, 0)),
        out_shape=jax.ShapeDtypeStruct((nb * t, D_FOX), F32),
        scratch_shapes=_flash_scratch(2 * tq, tq),
        compiler_params=_params("parallel", "parallel"),
        name="fox_prompt",
    )(q2d, k_t, v_t, bias)


def _merge_kernel(x_ref, oc_ref, os_ref, ow_ref, misc_ref, of_ref, ga_ref, gb_ref, woa_ref, wob_ref, wout_ref, h_ref):
    tm = x_ref.shape[0]
    misc = misc_ref[...]
    lo_lanes = _lane_iota((tm, LANES)) < HEAD_DIM

    def gate(head, j):
        c = 3 * head + j
        return jnp.broadcast_to(misc[:, c:c + 1], (tm, LANES))

    chunks = []
    for p in range(NSA_GROUP):
        cols = slice(p * LANES, (p + 1) * LANES)
        acc = None
        for j, o_ref in enumerate((oc_ref, os_ref, ow_ref)):
            term = jnp.where(lo_lanes, gate(p, j), gate(NSA_GROUP + p, j)) * o_ref[:, cols]
            acc = term if acc is None else acc + term
        chunks.append(acc.astype(BF16))
    ya = _dot(jnp.concatenate(chunks, axis=1), woa_ref[...])
    yb = _dot(of_ref[...].astype(BF16), wob_ref[...])
    mix = (ga_ref[...] * ya + gb_ref[...] * yb).astype(BF16)
    h_ref[...] = x_ref[...] + _dot(mix, wout_ref[...])


def _merge(x2d, o_cmp, o_sel, o_win, misc, o_fox, ga, gb, wts, tm):
    n = x2d.shape[0]
    assert n % tm == 0
    row = lambda w: pl.BlockSpec((tm, w), lambda i: (i, 0))
    par = [wts['w_o_a'], wts['w_o_b'], wts['w_out']]
    return pl.pallas_call(
        _merge_kernel,
        grid=(n // tm,),
        in_specs=[row(D_MODEL), row(D_QA), row(D_QA), row(D_QA), row(LANES), row(D_FOX), row(D_MODEL), row(D_MODEL)]
        + [_resident(a.shape) for a in par],
        out_specs=row(D_MODEL),
        out_shape=jax.ShapeDtypeStruct((n, D_MODEL), F32),
        compiler_params=_params("parallel"),
        name="merge",
    )(x2d, o_cmp, o_sel, o_win, misc, o_fox, ga, gb, *par)


FFN_CHUNK = 512


def _ffn_kernel(h_ref, p1_ref, p2_ref, g_ref, wup_ref, wgate_ref, wdown_ref, cw_ref, cb_ref, y_ref, tail_ref,
                carry_ref, *, tm, t_seq, carry):
    i = pl.program_id(0)
    h = h_ref[...]
    ms = jnp.mean(h * h, axis=-1, keepdims=True)
    hn = (h * lax.rsqrt(ms + EPS) * g_ref[...]).astype(BF16)
    t_in_seq = (i * tm + _row_iota((tm, FFN_CHUNK))) % t_seq
    acc = jnp.zeros((tm, D_MODEL), F32)
    for f in range(D_FF // FFN_CHUNK):
        cols = slice(f * FFN_CHUNK, (f + 1) * FFN_CHUNK)
        u = _dot(hn, wup_ref[:, cols])
        gt = _dot(hn, wgate_ref[:, cols])
        r1 = pltpu.roll(u, 1, 0)
        r2 = pltpu.roll(u, 2, 0)
        if carry:
            prev = carry_ref[:, cols]
            first = (i * tm) % t_seq == 0
            prev = jnp.where(first, jnp.zeros_like(prev), prev)
            row = _row_iota((tm, FFN_CHUNK))
            r1 = jnp.where(row == 0, prev[7:8, :], r1)
            r2 = jnp.where(row == 0, prev[6:7, :], jnp.where(row == 1, prev[7:8, :], r2))
            carry_ref[:, cols] = u[tm - 8:tm, :]
        else:
            r1 = jnp.where(t_in_seq >= 1, r1, p1_ref[:, cols])
            r2 = jnp.where(t_in_seq >= 2, r2, p2_ref[:, cols])
        uc = cb_ref[:, cols] + cw_ref[0:1, cols] * r2 + cw_ref[1:2, cols] * r1 + cw_ref[2:3, cols] * u
        acc = acc + _dot((_gelu(uc) * gt).astype(BF16), wdown_ref[cols, :])
        tail_ref[:, cols] = u[tm - tail_ref.shape[0]:tm, :]
    y_ref[...] = h + acc


def _ffn(h2d, p1, p2, wts, tm, t_seq, tail_rows):
    n = h2d.shape[0]
    assert n % tm == 0 and tail_rows % 8 == 0
    carry = p1 is None
    assert (t_seq % tm == 0) if carry else (tm % t_seq == 0)
    row = lambda w: pl.BlockSpec((tm, w), lambda i: (i, 0))
    par = [wts['g_ffn'], wts['w_up'], wts['w_gate'], wts['w_down'], wts['conv_w'], wts['conv_b']]
    if carry:
        kern = lambda h_ref, *rest, **kw: _ffn_kernel(h_ref, None, None, *rest, **kw)
        acts, act_specs = [h2d], [row(D_MODEL)]
    else:
        kern = _ffn_kernel
        acts, act_specs = [h2d, p1, p2], [row(D_MODEL), row(D_FF), row(D_FF)]
    return pl.pallas_call(
        functools.partial(kern, tm=tm, t_seq=t_seq, carry=carry),
        grid=(n // tm,),
        in_specs=act_specs + [_resident(a.shape) for a in par],
        out_specs=[row(D_MODEL), pl.BlockSpec((tail_rows, D_FF), lambda i: (i, 0))],
        out_shape=[jax.ShapeDtypeStruct((n, D_MODEL), F32),
                   jax.ShapeDtypeStruct((n // tm * tail_rows, D_FF), F32)],
        scratch_shapes=[pltpu.VMEM((8, D_FF), F32)],
        compiler_params=_params("arbitrary"),
        name="ffn",
    )(*acts, *par)


def _prep_weights(p):
    tile = lambda g, n: jnp.tile(g.astype(F32), n)[None, :]
    col = lambda g, n: jnp.tile(g.astype(F32), n)[:, None]
    w_tok, w_feat = _permute_w_in(p['w_in'])
    return dict(
        g_attn=p['norm_attn_g'].astype(F32)[None, :],
        w_tok=w_tok,
        w_feat=w_feat,
        m64=_head_mean_matrix(),
        gq=tile(p['q_norm_a_g'], NSA_HEADS) * ATTN_SCALE,
        gfq=tile(p['q_norm_b_g'], FOX_HEADS) * ATTN_SCALE,
        gkc=tile(p['k_norm_cmp_g'], NSA_KV_HEADS),
        gsk_col=col(p['k_norm_sel_g'], NSA_KV_HEADS),
        gwk_col=col(p['k_norm_win_g'], NSA_KV_HEADS),
        gfk_col=col(p['k_norm_b_g'], FOX_HEADS),
        bf_col=p['b_forget'].astype(F32)[:, None],
        cmpk=_cmp_weights(p['cmp_pe_k'], p['cmp_w1_k'], p['cmp_b1_k'], p['cmp_w2_k']),
        cmpv=_cmp_weights(p['cmp_pe_v'], p['cmp_w1_v'], p['cmp_b1_v'], p['cmp_w2_v']),
        w_o_a=p['w_o_a'][_qa_perm()].astype(BF16),
        w_o_b=p['w_o_b'].astype(BF16),
        w_out=p['w_out'].astype(BF16),
        g_ffn=p['norm_ffn_g'].astype(F32)[None, :],
        w_up=p['w_up'].astype(BF16),
        w_gate=p['w_gate'].astype(BF16),
        w_down=p['w_down'].astype(BF16),
        conv_w=p['conv_w'].astype(F32),
        conv_b=p['conv_b'].astype(F32)[None, :],
    )


Q_PAD = 16


SEL_HALVES = PAGE_SIZE // SEL_BLOCK


def _sel_sample_kernel(pt_ref, ix_ref, q_ref, *refs, n_cached, past):
    kb, vb = refs[:N_SEL], refs[N_SEL:2 * N_SEL]
    knew_ref, vnew_ref, o_ref = refs[2 * N_SEL:]
    b, t, g = pl.program_id(0), pl.program_id(1), pl.program_id(2)
    nk = (N_SEL + 1) * PAGE_SIZE
    kcat = jnp.concatenate([r[0, 0] for r in kb] + [knew_ref[0, 0]], axis=1).astype(BF16)
    vcat = jnp.concatenate([r[0, 0] for r in vb] + [vnew_ref[0, 0]], axis=1).astype(BF16)
    lane = _lane_iota((1, nk))
    slot = lane // PAGE_SIZE
    col = lane % PAGE_SIZE
    kpos = col + past
    valid = slot == N_SEL
    for j in range(N_SEL):
        blk = ix_ref[b, g * (ix_ref.shape[1] // NSA_KV_HEADS) + t * N_SEL + j]
        here = slot == j
        kpos = jnp.where(here, col + (blk // SEL_HALVES) * PAGE_SIZE, kpos)
        valid = valid | (here & (blk < n_cached) & ((col // SEL_BLOCK) == (blk % SEL_HALVES)))
    qpos = past + t
    dist = qpos - kpos
    mask = valid & (dist >= 0)
    row = _row_iota((Q_PAD, 1))
    slope = jnp.where(row == 0, 0.5, jnp.where(row == 1, 0.25, jnp.where(row == 2, 0.125, 0.0625)))
    slope = slope * jnp.where(g == 0, 1.0, 2.0 ** -NSA_GROUP)
    s = _dot(q_ref[0, 0, 0], kcat)
    p = _softmax_rows(s - slope * dist.astype(F32), jnp.broadcast_to(mask, s.shape))
    o_ref[0, 0, 0] = _dot_nt(p.astype(BF16), vcat)


def _sel_sample(qsel, pool_k, pool_v, knew, vnew, page_table, ix, past):
    nb, nt = qsel.shape[:2]
    n_cached = past // SEL_BLOCK

    def page_spec(j):
        def imap(b, t, g, pt, ixr):
            blk = jnp.minimum(ixr[b, g * (nt * N_SEL) + t * N_SEL + j], n_cached - 1)
            return (pt[b, blk // SEL_HALVES], g, 0, 0)
        return pl.BlockSpec((1, 1, HEAD_DIM, PAGE_SIZE), imap)

    new_spec = pl.BlockSpec((1, 1, HEAD_DIM, PAGE_SIZE), lambda b, t, g, pt, ixr: (b, g, 0, 0))
    q_spec = pl.BlockSpec((1, 1, 1, Q_PAD, HEAD_DIM), lambda b, t, g, pt, ixr: (b, t, g, 0, 0))
    return pl.pallas_call(
        functools.partial(_sel_sample_kernel, n_cached=n_cached, past=past),
        grid_spec=pltpu.PrefetchScalarGridSpec(
            num_scalar_prefetch=2,
            grid=(nb, nt, NSA_KV_HEADS),
            in_specs=[q_spec] + [page_spec(j) for j in range(N_SEL)] * 2 + [new_spec, new_spec],
            out_specs=q_spec,
        ),
        out_shape=jax.ShapeDtypeStruct((nb, nt, NSA_KV_HEADS, Q_PAD, HEAD_DIM), F32),
        compiler_params=_params("parallel", "parallel", "parallel"),
        name="sel_sample",
    )(page_table, ix, qsel, *([pool_k] * N_SEL), *([pool_v] * N_SEL), knew, vnew)


def _win_sample_kernel(q_ref, k_ref, v_ref, o_ref, *, past, w_buf):
    o_ref[...] = _window_core(q_ref[...], k_ref[0].astype(BF16), v_ref[0].astype(BF16), past, past - w_buf, Q_PAD)


def _win_sample(qpad, k_all, v_all, past, w_buf):
    nb, _, nk = k_all.shape
    return pl.pallas_call(
        functools.partial(_win_sample_kernel, past=past, w_buf=w_buf),
        grid=(nb,),
        in_specs=[pl.BlockSpec((Q_PAD, D_QA), lambda b: (b, 0)),
                  pl.BlockSpec((1, D_KVA, nk), lambda b: (b, 0, 0)),
                  pl.BlockSpec((1, D_KVA, nk), lambda b: (b, 0, 0))],
        out_specs=pl.BlockSpec((Q_PAD, D_QA), lambda b: (b, 0)),
        out_shape=jax.ShapeDtypeStruct((nb * Q_PAD, D_QA), F32),
        compiler_params=_params("parallel"),
        name="win_sample",
    )(qpad, k_all, v_all)


def _fox_decode_kernel(pt_ref, q_ref, *refs, pps, n_new):
    kp, vp, lp = refs[:pps], refs[pps:2 * pps], refs[2 * pps:3 * pps]
    knew_ref, vnew_ref, lnew_ref, o_ref, m_ref, l_ref, acc_ref, carry_ref = refs[3 * pps:]
    j = pl.program_id(1)
    rows = n_new * FOX_HEADS

    @pl.when(j == 0)
    def _():
        m_ref[...] = jnp.full(m_ref.shape, NEG_INF, F32)
        l_ref[...] = jnp.zeros(l_ref.shape, F32)
        acc_ref[...] = jnp.zeros(acc_ref.shape, F32)
        carry_ref[...] = jnp.zeros(carry_ref.shape, F32)

    q = q_ref[...].astype(F32)
    head_of_lane = _lane_iota((FOX_HEADS, D_FOX)) // HEAD_DIM
    own = head_of_lane == _row_iota((FOX_HEADS, D_FOX))
    wq = jnp.concatenate([jnp.where(own, jnp.broadcast_to(q[t:t + 1, :], (FOX_HEADS, D_FOX)), 0.0)
                          for t in range(n_new)], axis=0).astype(BF16)

    def update(s, mask, v):
        lg = s if mask is None else jnp.where(mask, s, NEG_INF)
        m_old = m_ref[...]
        m_new = jnp.maximum(m_old, jnp.max(lg, axis=-1, keepdims=True))
        p = jnp.exp(lg - m_new)
        if mask is not None:
            p = jnp.where(mask, p, 0.0)
        alpha = jnp.exp(m_old - m_new)
        l_ref[...] = alpha * l_ref[...] + jnp.sum(p, axis=-1, keepdims=True)
        m_ref[...] = m_new
        acc_ref[...] = alpha * acc_ref[...] + _dot_nt(p.astype(BF16), v)

    after = jnp.where(_row_iota((PAGE_SIZE, PAGE_SIZE)) > _lane_iota((PAGE_SIZE, PAGE_SIZE)), 1.0, 0.0).astype(BF16)
    carry = carry_ref[...]
    s_parts = []
    for u in range(pps):
        lft = lp[u][0]
        suffix = _dot3(lft, after) + carry[:, 0:1]
        carry = carry + jnp.sum(lft, axis=-1, keepdims=True)
        bias = jnp.concatenate([suffix] * n_new, axis=0)
        s_parts.append(_dot(wq, kp[u][0].astype(BF16)) + bias)
    carry_ref[...] = carry
    vcat = jnp.concatenate([r[0].astype(BF16) for r in vp], axis=1)
    update(jnp.concatenate(s_parts, axis=1), None, vcat)

    @pl.when(j == pl.num_programs(1) - 1)
    def _():
        upto = jnp.where(_row_iota((PAGE_SIZE, PAGE_SIZE)) <= _lane_iota((PAGE_SIZE, PAGE_SIZE)), 1.0, 0.0)
        c_new = _dot3(lnew_ref[0], upto.astype(BF16))
        s_new = _dot(wq, knew_ref[0].astype(BF16)) - jnp.concatenate([c_new] * n_new, axis=0)
        mask = _lane_iota((rows, PAGE_SIZE)) <= (_row_iota((rows, PAGE_SIZE)) // FOX_HEADS)
        update(s_new, mask, vnew_ref[0].astype(BF16))
        o = acc_ref[...] / jnp.maximum(l_ref[...], TINY)
        outs = [jnp.sum(jnp.where(own, o[t * FOX_HEADS:(t + 1) * FOX_HEADS], 0.0), axis=0, keepdims=True)
                for t in range(n_new)]
        o_ref[...] = jnp.concatenate(outs + [jnp.zeros((Q_PAD - n_new, D_FOX), F32)], axis=0)


def _fox_decode(qpad, pool_k, pool_v, pool_lft, knew, vnew, lnew_t, page_table, n_new, pps):
    nb, n_pages = page_table.shape
    assert n_pages % pps == 0
    steps = n_pages // pps

    def page_spec(u, shape):
        return pl.BlockSpec((1,) + shape, lambda b, j, pt: (pt[b, n_pages - 1 - (j * pps + u)], 0, 0))

    per_b = lambda shape: pl.BlockSpec((1,) + shape, lambda b, j, pt: (b, 0, 0))
    rows = n_new * FOX_HEADS
    return pl.pallas_call(
        functools.partial(_fox_decode_kernel, pps=pps, n_new=n_new),
        grid_spec=pltpu.PrefetchScalarGridSpec(
            num_scalar_prefetch=1,
            grid=(nb, steps),
            in_specs=[pl.BlockSpec((Q_PAD, D_FOX), lambda b, j, pt: (b, 0))]
            + [page_spec(u, (D_FOX, PAGE_SIZE)) for u in range(pps)] * 2
            + [page_spec(u, (FOX_HEADS, PAGE_SIZE)) for u in range(pps)]
            + [per_b((D_FOX, PAGE_SIZE)), per_b((D_FOX, PAGE_SIZE)), per_b((FOX_HEADS, PAGE_SIZE))],
            out_specs=pl.BlockSpec((Q_PAD, D_FOX), lambda b, j, pt: (b, 0)),
            scratch_shapes=[pltpu.VMEM((rows, 1), F32), pltpu.VMEM((rows, 1), F32), pltpu.VMEM((rows, D_FOX), F32),
                            pltpu.VMEM((FOX_HEADS, 1), F32)],
        ),
        out_shape=jax.ShapeDtypeStruct((nb * Q_PAD, D_FOX), F32),
        compiler_params=_params("parallel", "arbitrary"),
        name="fox_decode",
    )(page_table, qpad, *([pool_k] * pps), *([pool_v] * pps), *([pool_lft] * pps), knew, vnew, lnew_t)


def _pad_rows(a, rows):
    return jnp.pad(a, ((0, 0), (0, rows - a.shape[1]), (0, 0)))


def _sample_layer(x, cache, page_table, wts):
    (c_cmp_k, c_cmp_v, c_sel_k, c_sel_v, c_fox_k, c_fox_v, c_fox_logf, s_win_k, s_win_v, s_conv) = cache
    nb, t, _ = x.shape
    n = nb * t
    n_pool = c_cmp_k.shape[0]
    n_pages = page_table.shape[1]
    past = n_pages * PAGE_SIZE
    x2d = x.reshape(n, D_MODEL)
    pr = _project(x2d, wts, 1, n, n)
    per_seq = lambda a: a.reshape(nb, t, a.shape[-1])
    qpad = _pad_rows(per_seq(pr['qa']), Q_PAD).reshape(nb * Q_PAD, D_QA)
    take = lambda o: o.reshape(nb, Q_PAD, -1)[:, :t].reshape(n, -1)
    new_t = lambda a: jnp.swapaxes(a[0].reshape(a.shape[1], nb, t), 0, 1)
    lane_pad = lambda a, w: jnp.pad(a, ((0, 0), (0, 0), (0, w - a.shape[2])))
    paged = lambda pool: jnp.moveaxis(pool, 1, -1)

    nc = (past + t) // CMP_STRIDE
    assert nc * CMP_STRIDE <= past
    lhk, lhv = _cmp_lh_paged(c_cmp_k.reshape(n_pool, PAGE_SIZE, D_KVA), c_cmp_v.reshape(n_pool, PAGE_SIZE, D_KVA),
                             page_table, wts['cmpk'], wts['cmpv'], min(8, n_pages))
    kc, vc = _cmp_finish(lhk, lhv, wts['cmpk'], wts['cmpv'], wts['m64'], wts['gkc'], nc)
    n_slc = -(-(past + t) // SEL_BLOCK)
    o_cmp, idx = _cmp_attend(qpad, kc, vc, nb, Q_PAD, Q_PAD, nc, n_slc, past, 'idx')

    ix = idx[:, :, :t, :N_SEL].reshape(nb, NSA_KV_HEADS * t * N_SEL)
    qsel = jnp.swapaxes(pr['qa'].reshape(nb, t, NSA_GROUP, NSA_KV_HEADS, HEAD_DIM), 2, 3)
    qsel = jnp.pad(qsel, ((0, 0), (0, 0), (0, 0), (0, Q_PAD - NSA_GROUP), (0, 0)))
    grouped = lambda a: lane_pad(new_t(a), PAGE_SIZE).reshape(nb, NSA_KV_HEADS, HEAD_DIM, PAGE_SIZE)
    o_sel = _sel_sample(qsel, paged(c_sel_k), paged(c_sel_v), grouped(pr['skT']), grouped(pr['svT']),
                        page_table, ix, past)
    o_sel = jnp.swapaxes(o_sel[:, :, :, :NSA_GROUP], 2, 3).reshape(n, D_QA)

    w_buf = s_win_k.shape[1]
    wk_all = jnp.concatenate([paged(s_win_k).reshape(nb, D_KVA, w_buf), new_t(pr['wkT'])], axis=2)
    wv_all = jnp.concatenate([paged(s_win_v).reshape(nb, D_KVA, w_buf), new_t(pr['wvT'])], axis=2)
    o_win = _win_sample(qpad, lane_pad(wk_all, w_buf + LANES), lane_pad(wv_all, w_buf + LANES), past, w_buf)

    fqpad = _pad_rows(per_seq(pr['fq']), Q_PAD).reshape(nb * Q_PAD, D_FOX)
    o_fox = _fox_decode(fqpad, paged(c_fox_k).reshape(n_pool, D_FOX, PAGE_SIZE),
                        paged(c_fox_v).reshape(n_pool, D_FOX, PAGE_SIZE), paged(c_fox_logf),
                        lane_pad(new_t(pr['fkT']), PAGE_SIZE), lane_pad(new_t(pr['fvT']), PAGE_SIZE),
                        lane_pad(new_t(pr['logfT']), PAGE_SIZE), page_table, t, min(8, n_pages))

    h = _merge(x2d, take(o_cmp), o_sel, take(o_win), pr['gates'], take(o_fox), pr['ga'], pr['gb'], wts, n)
    zeros = jnp.zeros((nb, D_FF), F32)
    prev = s_conv.astype(F32)
    p1 = jnp.stack([prev[:, 1]] + [zeros] * (t - 1), axis=1).reshape(n, D_FF)
    p2 = jnp.stack([prev[:, 0], prev[:, 1]] + [zeros] * (t - 2), axis=1).reshape(n, D_FF)
    y, u = _ffn(h, p1, p2, wts, n, t, n)
    conv_rows = u.reshape(nb, t, D_FF)[:, t - (CONV_W - 1):]
    rows = lambda a, heads: a[0].T.reshape(nb, t, heads, HEAD_DIM)
    win = lambda a: jnp.moveaxis(a[:, :, t:t + w_buf].reshape(nb, NSA_KV_HEADS, HEAD_DIM, w_buf), -1, 1)
    state = (rows(pr['ckT'], NSA_KV_HEADS), rows(pr['cvT'], NSA_KV_HEADS), rows(pr['skT'], NSA_KV_HEADS),
             rows(pr['svT'], NSA_KV_HEADS), rows(pr['fkT'], FOX_HEADS), rows(pr['fvT'], FOX_HEADS),
             pr['logfT'][0].T.reshape(nb, t, FOX_HEADS), win(wk_all), win(wv_all), conv_rows)
    return y.reshape(nb, t, D_MODEL), state


def _tile_rows(n, pref):
    return pref if n % pref == 0 else n


def _prompt_layer(x, wts):
    nb, t, _ = x.shape
    n = nb * t
    x2d = x.reshape(n, D_MODEL)
    pr = _project(x2d, wts, nb, t, _tile_rows(t, 256))
    nc = t // CMP_STRIDE
    lhk, lhv = _cmp_lh_dense(pr['ck'].reshape(n // CMP_STRIDE, CHUNK_LANES), pr['cv'].reshape(n // CMP_STRIDE, CHUNK_LANES),
                             wts['cmpk'], wts['cmpv'], _tile_rows(n // CMP_STRIDE, 256))
    kc, vc = _cmp_finish(lhk, lhv, wts['cmpk'], wts['cmpv'], wts['m64'], wts['gkc'], nc)
    n_slc = -(-t // SEL_BLOCK)
    o_cmp, qaux = _cmp_attend(pr['qa'], kc, vc, nb, t, _tile_rows(t, 256), nc, n_slc, 0, 'features')
    o_sel = _sel_prompt(pr['qa'], pr['skTb'], pr['svTb'], qaux, nb, t, _tile_rows(t, 256), min(512, t))
    o_win = _win_prompt(pr['qa'], pr['wkTb'], pr['wvTb'], nb, t, 128)
    bias = _fox_bias(pr['logfT'], nb, t)
    o_fox = _fox_prompt(pr['fq'], pr['fkTb'], pr['fvTb'], bias, nb, t, min(512, t))
    h = _merge(x2d, o_cmp, o_sel, o_win, pr['gates'], o_fox, pr['ga'], pr['gb'], wts, _tile_rows(n, 256))
    tm = _tile_rows(t, 256)
    y, tail = _ffn(h, None, None, wts, tm, t, 8)
    conv_rows = tail.reshape(nb, t // tm, 8, D_FF)[:, -1, 8 - (CONV_W - 1):]
    rows = lambda a, heads: jnp.moveaxis(a.reshape(nb, heads, HEAD_DIM, a.shape[-1]), -1, 1)
    w_keep = min(WINDOW, t)
    state = (rows(pr['ckT'], NSA_KV_HEADS), rows(pr['cvT'], NSA_KV_HEADS), rows(pr['skT'], NSA_KV_HEADS),
             rows(pr['svT'], NSA_KV_HEADS), rows(pr['fkT'], FOX_HEADS), rows(pr['fvT'], FOX_HEADS),
             jnp.swapaxes(pr['logfT'], 1, 2), rows(pr['wkT'][:, :, t - w_keep:], NSA_KV_HEADS),
             rows(pr['wvT'][:, :, t - w_keep:], NSA_KV_HEADS), conv_rows)
    return y.reshape(nb, t, D_MODEL), state


_WEIGHT_NAMES = ('norm_attn_g', 'w_in', 'b_forget', 'q_norm_a_g', 'k_norm_cmp_g', 'k_norm_sel_g', 'k_norm_win_g',
                 'cmp_pe_k', 'cmp_w1_k', 'cmp_b1_k', 'cmp_w2_k', 'cmp_pe_v', 'cmp_w1_v', 'cmp_b1_v', 'cmp_w2_v',
                 'q_norm_b_g', 'k_norm_b_g', 'w_o_a', 'w_o_b', 'w_out', 'norm_ffn_g', 'w_up', 'w_gate', 'conv_w',
                 'conv_b', 'w_down')


def kernel(x_prompt, x_sample, cache_cmp_k, cache_cmp_v, cache_sel_k, cache_sel_v, cache_fox_k, cache_fox_v,
           cache_fox_logf, state_win_k, state_win_v, state_ffn_conv, page_table,
           norm_attn_g, w_in, b_forget, q_norm_a_g, k_norm_cmp_g, k_norm_sel_g, k_norm_win_g,
           cmp_pe_k, cmp_w1_k, cmp_b1_k, cmp_w2_k, cmp_pe_v, cmp_w1_v, cmp_b1_v, cmp_w2_v,
           q_norm_b_g, k_norm_b_g, w_o_a, w_o_b, w_out, norm_ffn_g, w_up, w_gate, conv_w, conv_b, w_down):
    weights = (norm_attn_g, w_in, b_forget, q_norm_a_g, k_norm_cmp_g, k_norm_sel_g, k_norm_win_g,
               cmp_pe_k, cmp_w1_k, cmp_b1_k, cmp_w2_k, cmp_pe_v, cmp_w1_v, cmp_b1_v, cmp_w2_v,
               q_norm_b_g, k_norm_b_g, w_o_a, w_o_b, w_out, norm_ffn_g, w_up, w_gate, conv_w, conv_b, w_down)
    caches = (cache_cmp_k, cache_cmp_v, cache_sel_k, cache_sel_v, cache_fox_k, cache_fox_v, cache_fox_logf,
              state_win_k, state_win_v, state_ffn_conv)
    depth = w_in.shape[0]
    prompt_states, sample_states = [], []
    for layer in range(depth):
        wts = _prep_weights({name: w[layer] for name, w in zip(_WEIGHT_NAMES, weights)})
        x_prompt, st_p = _prompt_layer(x_prompt, wts)
        x_sample, st_s = _sample_layer(x_sample, tuple(c[layer] for c in caches), page_table, wts)
        prompt_states.append(st_p)
        sample_states.append(st_s)
    new_prompt = [jnp.stack(rows) for rows in zip(*prompt_states)]
    new_sample = [jnp.stack(rows) for rows in zip(*sample_states)]
    return (x_prompt, x_sample, *new_prompt, *new_sample)
```

```python
import functools

import numpy as np
import jax
import jax.numpy as jnp
from jax import lax
from jax.experimental import pallas as pl
from jax.experimental.pallas import tpu as pltpu

F32 = jnp.float32
BF16 = jnp.bfloat16

D_MODEL = 1024
HEAD_DIM = 64
NSA_HEADS = 8
NSA_KV_HEADS = 2
NSA_GROUP = NSA_HEADS // NSA_KV_HEADS
FOX_HEADS = 8
CMP_BLOCK = 32
CMP_STRIDE = 16
CMP_HIDDEN = 2 * HEAD_DIM
SEL_BLOCK = 64
N_SEL = 16
WINDOW = 512
D_FF = 3 * D_MODEL
CONV_W = 3
PAGE_SIZE = 128
D_QA = NSA_HEADS * HEAD_DIM
D_KVA = NSA_KV_HEADS * HEAD_DIM
D_FOX = FOX_HEADS * HEAD_DIM
IN_SIZES = (D_QA, D_KVA, D_KVA, D_KVA, D_KVA, D_KVA, D_KVA, 3 * NSA_HEADS, D_FOX, D_FOX, D_FOX, FOX_HEADS,
            D_MODEL, D_MODEL)
ATTN_SCALE = HEAD_DIM ** -0.5
FORCED_SCORE = 1e6
NEG_INF = -1e30
TINY = 1e-30
EPS = 1e-6
LANES = 128
N_GATE = 3 * NSA_HEADS
VMEM_LIMIT = 56 * 1024 * 1024


def _dot(a, b):
    return jnp.dot(a, b, preferred_element_type=F32)


def _dot_nt(a, b):
    return lax.dot_general(a, b, (((1,), (1,)), ((), ())), preferred_element_type=F32)


def _params(*sem):
    return pltpu.CompilerParams(dimension_semantics=sem, vmem_limit_bytes=VMEM_LIMIT)


def _resident(shape):
    nd = len(shape)
    return pl.BlockSpec(shape, lambda *_: (0,) * nd, pipeline_mode=pl.Buffered(1))


def _lane_iota(shape):
    return lax.broadcasted_iota(jnp.int32, shape, len(shape) - 1)


def _row_iota(shape):
    return lax.broadcasted_iota(jnp.int32, shape, len(shape) - 2)


def _gelu(x):
    return 0.5 * x * (1.0 + jnp.tanh(0.7978845608028654 * (x + 0.044715 * (x * x * x))))


def _sigmoid(x):
    return 1.0 / (1.0 + jnp.exp(-x))


def _log_sigmoid(x):
    return -(jnp.maximum(-x, 0.0) + jnp.log1p(jnp.exp(-jnp.abs(x))))


def _softmax_rows(logits, mask):
    lg = jnp.where(mask, logits, NEG_INF)
    m = jnp.max(lg, axis=-1, keepdims=True)
    e = jnp.where(mask, jnp.exp(lg - m), 0.0)
    return e / jnp.maximum(jnp.sum(e, axis=-1, keepdims=True), TINY)


_C_QA = 0
_C_CK = _C_QA + D_QA
_C_CV = _C_CK + D_KVA
_C_FQ = _C_CV + D_KVA
_C_GA = _C_FQ + D_FOX
_C_GB = _C_GA + D_MODEL
_C_MISC = _C_GB + D_MODEL
_C_END = _C_MISC + LANES
_R_KV = 0
_R_FK = _R_KV + 6 * D_KVA
_R_FV = _R_FK + D_FOX
_R_FF = _R_FV + D_FOX
_R_END = _R_FF + 16


def _qa_perm():
    cols = []
    for p in range(NSA_GROUP):
        cols += list(range(p * HEAD_DIM, (p + 1) * HEAD_DIM))
        cols += list(range((NSA_GROUP + p) * HEAD_DIM, (NSA_GROUP + p + 1) * HEAD_DIM))
    return np.asarray(cols, np.int32)


def _permute_w_in(w_in):
    offs = [int(o) for o in np.concatenate([[0], np.cumsum(IN_SIZES)])]
    o_ck, o_sk, o_ng, o_fq, o_fk, o_ff, o_ga = offs[1], offs[3], offs[7], offs[8], offs[9], offs[11], offs[12]
    cols = np.concatenate([_qa_perm(), np.arange(o_ck, o_sk), np.arange(o_fq, o_fk), np.arange(o_ga, offs[-1]),
                           np.arange(o_ng, o_fq)])
    pad = jnp.zeros((w_in.shape[0], LANES - N_GATE), w_in.dtype)
    w_tok = jnp.concatenate([w_in[:, cols], pad], axis=1).astype(BF16)
    rows = np.concatenate([np.arange(o_ck, o_ng), np.arange(o_fk, o_ff), np.arange(o_ff, o_ga)])
    w_t = w_in.T
    w_feat = jnp.concatenate([w_t[rows], jnp.zeros((_R_END - len(rows), w_in.shape[0]), w_in.dtype)], axis=0)
    return w_tok, w_feat.astype(BF16)


def _head_mean_matrix():
    i = np.arange(D_QA)
    return jnp.asarray((i[:, None] // HEAD_DIM == i[None, :] // HEAD_DIM) / HEAD_DIM, BF16)


_PROJ_TOKEN_OUTS = (('qa', D_QA, BF16), ('ck', D_KVA, F32), ('cv', D_KVA, F32), ('fq', D_FOX, BF16),
                    ('ga', D_MODEL, F32), ('gb', D_MODEL, F32), ('gates', LANES, F32))
_PROJ_FEATURE_OUTS = (('ckT', D_KVA, F32), ('cvT', D_KVA, F32), ('skT', D_KVA, F32), ('svT', D_KVA, F32),
                      ('wkT', D_KVA, F32), ('wvT', D_KVA, F32), ('fkT', D_FOX, F32), ('fvT', D_FOX, F32),
                      ('logfT', FOX_HEADS, F32), ('skTb', D_KVA, BF16), ('svTb', D_KVA, BF16), ('wkTb', D_KVA, BF16),
                      ('wvTb', D_KVA, BF16), ('fkTb', D_FOX, BF16), ('fvTb', D_FOX, BF16))


def _proj_kernel(x_ref, gattn_ref, w_ref, wt_ref, m64_ref, gq_ref, gfq_ref, gsk_ref, gwk_ref, gfk_ref, bf_ref,
                 qa_ref, ck_ref, cv_ref, fq_ref, ga_ref, gb_ref, gates_ref,
                 ckT_ref, cvT_ref, skT_ref, svT_ref, wkT_ref, wvT_ref, fkT_ref, fvT_ref, logfT_ref,
                 skTb_ref, svTb_ref, wkTb_ref, wvTb_ref, fkTb_ref, fvTb_ref):
    x = x_ref[...]
    ms = jnp.mean(x * x, axis=-1, keepdims=True)
    xn = (x * lax.rsqrt(ms + EPS) * gattn_ref[...]).astype(BF16)

    def seg(a, width):
        return _dot(xn, w_ref[:, a:a + width])

    def seg_t(a, width):
        return _dot_nt(wt_ref[a:a + width, :], xn)

    def head_norm(z, g_ref):
        w = z.shape[-1]
        hm = _dot((z * z).astype(BF16), m64_ref[0:w, 0:w])
        return z * lax.rsqrt(hm + EPS) * g_ref[...]

    def store_t(z, f32_ref, bf_ref_, g_ref):
        for h in range(z.shape[0] // HEAD_DIM):
            rows = slice(h * HEAD_DIM, (h + 1) * HEAD_DIM)
            zh = z[rows]
            if g_ref is not None:
                zh = zh * lax.rsqrt(jnp.mean(zh * zh, axis=0, keepdims=True) + EPS) * g_ref[rows, :]
            f32_ref[0, rows, :] = zh
            if bf_ref_ is not None:
                bf_ref_[0, rows, :] = zh.astype(BF16)

    qa_ref[...] = head_norm(seg(_C_QA, D_QA), gq_ref).astype(BF16)
    ck_ref[...] = seg(_C_CK, D_KVA)
    cv_ref[...] = seg(_C_CV, D_KVA)
    fq_ref[...] = head_norm(seg(_C_FQ, D_FOX), gfq_ref).astype(BF16)
    ga_ref[...] = _sigmoid(seg(_C_GA, D_MODEL))
    gb_ref[...] = _sigmoid(seg(_C_GB, D_MODEL))
    gates_ref[...] = _sigmoid(seg(_C_MISC, LANES))

    store_t(seg_t(_R_KV, D_KVA), ckT_ref, None, None)
    store_t(seg_t(_R_KV + D_KVA, D_KVA), cvT_ref, None, None)
    store_t(seg_t(_R_KV + 2 * D_KVA, D_KVA), skT_ref, skTb_ref, gsk_ref)
    store_t(seg_t(_R_KV + 3 * D_KVA, D_KVA), svT_ref, svTb_ref, None)
    store_t(seg_t(_R_KV + 4 * D_KVA, D_KVA), wkT_ref, wkTb_ref, gwk_ref)
    store_t(seg_t(_R_KV + 5 * D_KVA, D_KVA), wvT_ref, wvTb_ref, None)
    store_t(seg_t(_R_FK, D_FOX), fkT_ref, fkTb_ref, gfk_ref)
    store_t(seg_t(_R_FV, D_FOX), fvT_ref, fvTb_ref, None)
    logfT_ref[0] = _log_sigmoid(seg_t(_R_FF, _R_END - _R_FF)[0:FOX_HEADS] + bf_ref[...])


def _project(x2d, wts, nb, t, tm):
    assert t % tm == 0
    nt = t // tm
    row = lambda w: pl.BlockSpec((tm, w), lambda i: (i, 0))
    feat = lambda w: pl.BlockSpec((1, w, tm), lambda i: (i // nt, 0, i % nt))
    small = [wts['g_attn'], wts['w_tok'], wts['w_feat'], wts['m64'], wts['gq'], wts['gfq'], wts['gsk_col'],
             wts['gwk_col'], wts['gfk_col'], wts['bf_col']]
    res = pl.pallas_call(
        _proj_kernel,
        grid=(nb * nt,),
        in_specs=[row(D_MODEL)] + [_resident(a.shape) for a in small],
        out_specs=[row(w) for _, w, _ in _PROJ_TOKEN_OUTS] + [feat(w) for _, w, _ in _PROJ_FEATURE_OUTS],
        out_shape=[jax.ShapeDtypeStruct((nb * t, w), dt) for _, w, dt in _PROJ_TOKEN_OUTS]
        + [jax.ShapeDtypeStruct((nb, w, t), dt) for _, w, dt in _PROJ_FEATURE_OUTS],
        compiler_params=_params("parallel"),
        name="proj",
    )(x2d, *small)
    return dict(zip([n for n, _, _ in _PROJ_TOKEN_OUTS + _PROJ_FEATURE_OUTS], res))


CHUNK_LANES = CMP_STRIDE * D_KVA
CHUNKS_PER_PAGE = PAGE_SIZE // CMP_STRIDE


def _cmp_weights(pe, w1, b1, w2):
    eye = jnp.eye(NSA_KV_HEADS, dtype=w1.dtype)
    big = lambda w: jnp.einsum('pdh,gk->pgdkh', w, eye).reshape(CHUNK_LANES, NSA_KV_HEADS * CMP_HIDDEN).astype(BF16)
    flat = lambda e: jnp.broadcast_to(e[:, None, :], (CMP_STRIDE, NSA_KV_HEADS, HEAD_DIM)).reshape(1, CHUNK_LANES)
    w2bd = jnp.einsum('he,gk->ghke', w2, eye).reshape(NSA_KV_HEADS * CMP_HIDDEN, D_KVA).astype(BF16)
    return dict(w1lo=big(w1[:CMP_STRIDE]), w1hi=big(w1[CMP_STRIDE:]), pelo=flat(pe[:CMP_STRIDE]),
                pehi=flat(pe[CMP_STRIDE:]), b1=jnp.tile(b1, NSA_KV_HEADS)[None, :], w2=w2bd)


def _cmp_lh_body(x, pelo, pehi, w1lo, w1hi):
    lo = _dot((x + pelo).astype(BF16), w1lo)
    hi = _dot((x + pehi).astype(BF16), w1hi)
    return jnp.concatenate([lo, hi], axis=1)


def _cmp_lh_kernel(*refs, n_blocks, paged):
    xk = refs[:n_blocks]
    xv = refs[n_blocks:2 * n_blocks]
    pk = refs[2 * n_blocks:2 * n_blocks + 4]
    pv = refs[2 * n_blocks + 4:2 * n_blocks + 8]
    ok_ref, ov_ref = refs[2 * n_blocks + 8:]

    def rows(blocks):
        if not paged:
            return blocks[0][...]
        return jnp.concatenate(
            [jnp.concatenate([b[0, pl.ds(p, CHUNKS_PER_PAGE, stride=CMP_STRIDE), :] for p in range(CMP_STRIDE)],
                             axis=1) for b in blocks], axis=0)

    ok_ref[...] = _cmp_lh_body(rows(xk), pk[0][...], pk[1][...], pk[2][...], pk[3][...])
    ov_ref[...] = _cmp_lh_body(rows(xv), pv[0][...], pv[1][...], pv[2][...], pv[3][...])


def _cmp_lh_dense(xk, xv, cwk, cwv, tm):
    n = xk.shape[0]
    assert n % tm == 0
    row = lambda w: pl.BlockSpec((tm, w), lambda i: (i, 0))
    par = [cwk['pelo'], cwk['pehi'], cwk['w1lo'], cwk['w1hi'], cwv['pelo'], cwv['pehi'], cwv['w1lo'], cwv['w1hi']]
    wlh = 2 * NSA_KV_HEADS * CMP_HIDDEN
    return pl.pallas_call(
        functools.partial(_cmp_lh_kernel, n_blocks=1, paged=False),
        grid=(n // tm,),
        in_specs=[row(CHUNK_LANES), row(CHUNK_LANES)] + [_resident(a.shape) for a in par],
        out_specs=[row(wlh), row(wlh)],
        out_shape=[jax.ShapeDtypeStruct((n, wlh), F32)] * 2,
        compiler_params=_params("parallel"),
        name="cmp_lh",
    )(xk, xv, *par)


def _cmp_lh_paged(pool_k, pool_v, page_table, cwk, cwv, pages_per_step):
    nb, n_pages = page_table.shape
    assert n_pages % pages_per_step == 0
    steps = n_pages // pages_per_step

    def page_spec(u):
        return pl.BlockSpec((1, PAGE_SIZE, D_KVA), lambda b, j, pt: (pt[b, j * pages_per_step + u], 0, 0))

    par = [cwk['pelo'], cwk['pehi'], cwk['w1lo'], cwk['w1hi'], cwv['pelo'], cwv['pehi'], cwv['w1lo'], cwv['w1hi']]
    wlh = 2 * NSA_KV_HEADS * CMP_HIDDEN
    rows = pages_per_step * CHUNKS_PER_PAGE
    out_spec = pl.BlockSpec((rows, wlh), lambda b, j, pt: (b * steps + j, 0))
    const = lambda a: pl.BlockSpec(a.shape, lambda b, j, pt: (0,) * a.ndim, pipeline_mode=pl.Buffered(1))
    def kern(pt_ref, *refs):
        _cmp_lh_kernel(*refs, n_blocks=pages_per_step, paged=True)

    return pl.pallas_call(
        kern,
        grid_spec=pltpu.PrefetchScalarGridSpec(
            num_scalar_prefetch=1,
            grid=(nb, steps),
            in_specs=[page_spec(u) for u in range(pages_per_step)] * 2 + [const(a) for a in par],
            out_specs=[out_spec, out_spec],
        ),
        out_shape=[jax.ShapeDtypeStruct((nb * n_pages * CHUNKS_PER_PAGE, wlh), F32)] * 2,
        compiler_params=_params("parallel", "parallel"),
        name="cmp_lh_paged",
    )(page_table, *([pool_k] * pages_per_step), *([pool_v] * pages_per_step), *par)


def _cmp_finish_kernel(lhk_ref, lhv_ref, b1k_ref, w2k_ref, b1v_ref, w2v_ref, m64_ref, gk_ref, kc_ref, vc_ref):
    def finish(lh_ref, b1_ref, w2_ref):
        lh = lh_ref[...]
        nc = lh.shape[0]
        half = lh.shape[1] // 2
        hi_next = pltpu.roll(lh[:, half:], nc - 1, 0)
        h = _gelu(lh[:, :half] + hi_next + b1_ref[...])
        return _dot(h.astype(BF16), w2_ref[...])

    kc = finish(lhk_ref, b1k_ref, w2k_ref)
    hm = _dot((kc * kc).astype(BF16), m64_ref[0:D_KVA, 0:D_KVA])
    kc_ref[...] = (kc * lax.rsqrt(hm + EPS) * gk_ref[...]).astype(BF16)
    vc_ref[...] = finish(lhv_ref, b1v_ref, w2v_ref).astype(BF16)


def _cmp_finish(lhk, lhv, cwk, cwv, m64, gk, nc):
    n = lhk.shape[0]
    nb = n // nc
    seq = lambda w: pl.BlockSpec((nc, w), lambda b: (b, 0))
    par = [cwk['b1'], cwk['w2'], cwv['b1'], cwv['w2'], m64, gk]
    return pl.pallas_call(
        _cmp_finish_kernel,
        grid=(nb,),
        in_specs=[seq(lhk.shape[1]), seq(lhv.shape[1])] + [_resident(a.shape) for a in par],
        out_specs=[seq(D_KVA), seq(D_KVA)],
        out_shape=[jax.ShapeDtypeStruct((n, D_KVA), BF16)] * 2,
        compiler_params=_params("parallel"),
        name="cmp_finish",
    )(lhk, lhv, *par)


def _alibi_slope(g, r):
    return 2.0 ** -(NSA_GROUP * g + r + 1)


def _stack_heads(q):
    return jnp.concatenate([q[:, p * LANES:(p + 1) * LANES] for p in range(NSA_GROUP)], axis=0)


def _group_lanes(x, g):
    return jnp.where((_lane_iota(x.shape) // HEAD_DIM) == g, x, jnp.zeros_like(x))


def _unstack_heads(o0, o1, tq):
    lo = _lane_iota((tq, LANES)) < HEAD_DIM
    return jnp.concatenate([jnp.where(lo, o0[p * tq:(p + 1) * tq], o1[p * tq:(p + 1) * tq])
                            for p in range(NSA_GROUP)], axis=1)


MASK_BIAS = -1e9


def _cmp_group(qs, kc, vc, g, distf, mask, tq):
    s = _dot_nt(qs, _group_lanes(kc, g))
    ps = [_softmax_rows(s[r * tq:(r + 1) * tq] - _alibi_slope(g, r) * distf, mask) for r in range(NSA_GROUP)]
    out = _dot(jnp.concatenate(ps, axis=0).astype(BF16), vc)
    return out, (ps[0] + ps[1]) + (ps[2] + ps[3])


def _split2(x):
    hi = x.astype(BF16)
    return hi, (x - hi.astype(F32)).astype(BF16)


def _block_scores(p_slc, blk, qp, n_slc):
    cur = lax.shift_right_logical(qp, 6)
    forced = (blk == 0) | (blk == cur) | (blk == cur - 1)
    score = jnp.where(forced, FORCED_SCORE, jnp.where(blk * SEL_BLOCK <= qp, p_slc, -1.0))
    return jnp.where(blk < n_slc, score, -2.0), cur


def _cmp_dist(qpos0, tq, nck):
    dist = (qpos0 + _row_iota((tq, nck))) - (_lane_iota((tq, nck)) * CMP_STRIDE + (CMP_BLOCK - 1))
    return dist.astype(F32), dist >= 0


def _cmp_attn_idx_kernel(q_ref, kc_ref, vc_ref, o_ref, idx_ref, *, tq, n_slc, pos0):
    nck = kc_ref.shape[0]
    nsl = -(-n_slc // LANES) * LANES
    qpos0 = pos0 + pl.program_id(1) * tq
    qs = _stack_heads(q_ref[...])
    distf, mask = _cmp_dist(qpos0, tq, nck)
    c_start = _row_iota((nck, nsl)) * CMP_STRIDE
    s_start = _lane_iota((nck, nsl)) * SEL_BLOCK
    overlap = jnp.where((c_start < s_start + SEL_BLOCK) & (c_start + CMP_BLOCK > s_start), 1.0, 0.0).astype(BF16)
    blk = _lane_iota((tq, nsl))
    blkf = blk.astype(F32)
    slot = _lane_iota((tq, LANES))
    outs = []
    for g in range(NSA_KV_HEADS):
        out, psum = _cmp_group(qs, kc_ref[...], vc_ref[...], g, distf, mask, tq)
        outs.append(out)
        hi, lo = _split2(psum)
        score, _ = _block_scores(_dot(hi, overlap) + _dot(lo, overlap), blk, qpos0 + _row_iota((tq, nsl)), n_slc)
        idx = jnp.zeros((tq, LANES), F32)
        for it in range(N_SEL):
            m = jnp.max(score, axis=-1, keepdims=True)
            j = jnp.min(jnp.where(score == m, blkf, 1e9), axis=-1, keepdims=True)
            score = jnp.where(blkf == j, -3.0, score)
            idx = jnp.where(slot == it, j, idx)
        idx_ref[0, g] = idx.astype(jnp.int32)
    o_ref[...] = _unstack_heads(outs[0], outs[1], tq)


def _cmp_attn_feat_kernel(q_ref, kc_ref, vc_ref, o_ref, qaux_ref, *, tq, n_slc):
    nck = kc_ref.shape[0]
    rows = -(-n_slc // 8) * 8
    qpos0 = pl.program_id(1) * tq
    qs = _stack_heads(q_ref[...])
    distf, mask = _cmp_dist(qpos0, tq, nck)
    s_start = _row_iota((LANES, nck)) * SEL_BLOCK
    c_start = _lane_iota((LANES, nck)) * CMP_STRIDE
    overlap_t = jnp.where((c_start < s_start + SEL_BLOCK) & (c_start + CMP_BLOCK > s_start), 1.0, 0.0).astype(BF16)
    blk = _row_iota((rows, tq))
    qp = qpos0 + _lane_iota((rows, tq))
    outs = []
    for g in range(NSA_KV_HEADS):
        out, psum = _cmp_group(qs, kc_ref[...], vc_ref[...], g, distf, mask, tq)
        outs.append(out)
        hi, lo = _split2(psum)
        p_slc_t = _dot_nt(overlap_t, hi) + _dot_nt(overlap_t, lo)
        score, cur = _block_scores(p_slc_t[0:rows], blk, qp, n_slc)
        rank = jnp.zeros((rows, tq), F32)
        for i in range(n_slc):
            si = score[i:i + 1, :]
            rank = rank + jnp.where((si > score) | ((si == score) & (blk > i)), 1.0, 0.0)
        feat = jnp.where((rank < N_SEL) & (blk <= cur), 0.0, MASK_BIAS)
        feat = jnp.concatenate([feat, jnp.zeros((LANES - rows, tq), F32)], axis=0)
        qaux_ref[0, g] = feat.T.astype(BF16)
    o_ref[...] = _unstack_heads(outs[0], outs[1], tq)


def _cmp_attend(q2d, kc2d, vc2d, nb, t, tq, nck, n_slc, pos0, want):
    nt = t // tq
    if want == 'idx':
        kern = functools.partial(_cmp_attn_idx_kernel, tq=tq, n_slc=n_slc, pos0=pos0)
        second = jnp.int32
    else:
        assert pos0 == 0 and n_slc < AUX_BLOCK_HI
        kern = functools.partial(_cmp_attn_feat_kernel, tq=tq, n_slc=n_slc)
        second = BF16
    return pl.pallas_call(
        kern,
        grid=(nb, nt),
        in_specs=[pl.BlockSpec((tq, D_QA), lambda b, i: (b * nt + i, 0)),
                  pl.BlockSpec((nck, D_KVA), lambda b, i: (b, 0)),
                  pl.BlockSpec((nck, D_KVA), lambda b, i: (b, 0))],
        out_specs=[pl.BlockSpec((tq, D_QA), lambda b, i: (b * nt + i, 0)),
                   pl.BlockSpec((1, NSA_KV_HEADS, tq, LANES), lambda b, i: (b, 0, i, 0))],
        out_shape=[jax.ShapeDtypeStruct((nb * t, D_QA), F32),
                   jax.ShapeDtypeStruct((nb, NSA_KV_HEADS, t, LANES), second)],
        compiler_params=_params("parallel", "parallel"),
        name="cmp_attn",
    )(q2d, kc2d, vc2d)


FLASH_ROWS = 16
FLASH_UNROLL_MAX = 64
FLASH_UNROLL_EXP = 64


def _flash_scratch(rows, tk):
    return [pltpu.VMEM((rows, LANES), F32), pltpu.VMEM((rows, LANES), F32), pltpu.VMEM((rows, 2 * LANES), F32),
            pltpu.VMEM((rows, tk), F32), pltpu.VMEM((rows, tk), BF16)]


def _flash_init(m_ref, acc_ref):
    m_ref[...] = jnp.full(m_ref.shape, NEG_INF, F32)
    acc_ref[...] = jnp.zeros(acc_ref.shape, F32)


def _flash_update(s_ref, p_ref, mask_fn, v_t, m_ref, alpha_ref, acc_ref):
    rows, tk = s_ref.shape

    def masked(r0, sl):
        s = s_ref[sl, :]
        return s if mask_fn is None else jnp.where(mask_fn(r0), s, NEG_INF)

    def row_max(c, carry):
        r0 = pl.multiple_of(c * FLASH_ROWS, FLASH_ROWS)
        sl = pl.ds(r0, FLASH_ROWS)
        m_old = m_ref[sl, :]
        m_new = jnp.maximum(m_old, jnp.max(masked(r0, sl), axis=-1, keepdims=True))
        alpha_ref[sl, :] = jnp.exp(m_old - m_new)
        m_ref[sl, :] = m_new
        return carry

    def exponentiate(c, carry):
        r0 = pl.multiple_of(c * FLASH_ROWS, FLASH_ROWS)
        sl = pl.ds(r0, FLASH_ROWS)
        s = masked(r0, sl)
        m = m_ref[sl, :]
        for j in range(tk // LANES):
            cols = slice(j * LANES, (j + 1) * LANES)
            p_ref[sl, cols] = jnp.exp(s[:, cols] - m).astype(BF16)
        return carry

    lax.fori_loop(0, rows // FLASH_ROWS, row_max, 0, unroll=FLASH_UNROLL_MAX)
    lax.fori_loop(0, rows // FLASH_ROWS, exponentiate, 0, unroll=FLASH_UNROLL_EXP)
    v_aug = jnp.concatenate([v_t, jnp.ones_like(v_t)], axis=0)
    alpha = alpha_ref[...]
    acc_ref[...] = jnp.concatenate([alpha, alpha], axis=1) * acc_ref[...] + _dot_nt(p_ref[...], v_aug)


def _flash_result(acc_ref):
    acc = acc_ref[...]
    return acc[:, :LANES] / jnp.maximum(acc[:, LANES:], TINY)


def _causal_chunk_mask(qpos0, k0, tq, tk):
    assert tq & (tq - 1) == 0 and tq % FLASH_ROWS == 0

    def mask_fn(r0):
        qpos = qpos0 + (r0 & (tq - 1)) + _row_iota((FLASH_ROWS, tk))
        return (k0 + _lane_iota((FLASH_ROWS, tk))) <= qpos

    return mask_fn


AUX_BLOCK_HI = 120
AUX_BLOCK_LO = 121


def _sel_key_features(t):
    assert t // SEL_BLOCK <= AUX_BLOCK_HI
    s = np.arange(t)
    tab = np.zeros((LANES, t), np.float32)
    tab[s // SEL_BLOCK, s] = 1.0
    tab[AUX_BLOCK_HI] = s // SEL_BLOCK
    tab[AUX_BLOCK_LO] = s % SEL_BLOCK
    return jnp.asarray(tab, BF16)


def _sel_prompt_kernel(q_ref, k_ref, v_ref, kaux_ref, qaux_ref, o_ref, m_ref, alpha_ref, acc_ref, s_ref, p_ref, *,
                       tq, tk):
    i = pl.program_id(1)
    qpos0 = i * tq
    kt_diag = qpos0 // tk
    qs = _stack_heads(q_ref[...])
    lane = _lane_iota((tq, LANES))
    causal = _causal_chunk_mask(qpos0, kt_diag * tk, tq, tk)
    outs = []
    for g in range(NSA_KV_HEADS):
        feat = qaux_ref[0, g]
        aux = jnp.concatenate(
            [jnp.where(lane == AUX_BLOCK_HI, _alibi_slope(g, r) * SEL_BLOCK,
                       jnp.where(lane == AUX_BLOCK_LO, _alibi_slope(g, r), feat)) for r in range(NSA_GROUP)], axis=0)
        q_aug = jnp.concatenate([_group_lanes(qs, g), aux], axis=1)
        _flash_init(m_ref, acc_ref)

        def tile(kt, mask):
            k0 = pl.multiple_of(kt * tk, tk)
            k_aug = jnp.concatenate([k_ref[0, :, pl.ds(k0, tk)], kaux_ref[:, pl.ds(k0, tk)]], axis=0)
            s_ref[...] = _dot(q_aug, k_aug)
            _flash_update(s_ref, p_ref, mask, v_ref[0, :, pl.ds(k0, tk)], m_ref, alpha_ref, acc_ref)

        def body(kt, carry):
            tile(kt, None)
            return carry

        lax.fori_loop(0, kt_diag, body, 0)
        tile(kt_diag, causal)
        outs.append(_flash_result(acc_ref))
    o_ref[...] = _unstack_heads(outs[0], outs[1], tq)


def _sel_prompt(q2d, k_t, v_t, qaux, nb, t, tq, tk):
    assert tk % tq == 0 and t % tk == 0
    nt = t // tq
    return pl.pallas_call(
        functools.partial(_sel_prompt_kernel, tq=tq, tk=tk),
        grid=(nb, nt),
        in_specs=[pl.BlockSpec((tq, D_QA), lambda b, i: (b * nt + i, 0)),
                  pl.BlockSpec((1, D_KVA, t), lambda b, i: (b, 0, 0)),
                  pl.BlockSpec((1, D_KVA, t), lambda b, i: (b, 0, 0)),
                  _resident((LANES, t)),
                  pl.BlockSpec((1, NSA_KV_HEADS, tq, LANES), lambda b, i: (b, 0, i, 0))],
        out_specs=pl.BlockSpec((tq, D_QA), lambda b, i: (b * nt + i, 0)),
        out_shape=jax.ShapeDtypeStruct((nb * t, D_QA), F32),
        scratch_shapes=_flash_scratch(NSA_GROUP * tq, tk),
        compiler_params=_params("parallel", "parallel"),
        name="sel_prompt",
    )(q2d, k_t, v_t, _sel_key_features(t), qaux)


def _window_core(q, k_t, v_t, qpos0, kpos0, tq):
    nk = k_t.shape[1]
    qs = _stack_heads(q)
    kpos = kpos0 + _lane_iota((tq, nk))
    dist = (qpos0 + _row_iota((tq, nk))) - kpos
    mask = (dist >= 0) & (dist <= WINDOW) & (kpos >= 0)
    distf = dist.astype(F32)
    outs = []
    for g in range(NSA_KV_HEADS):
        s = _dot(_group_lanes(qs, g), k_t)
        ps = [_softmax_rows(s[r * tq:(r + 1) * tq] - _alibi_slope(g, r) * distf, mask).astype(BF16)
              for r in range(NSA_GROUP)]
        outs.append(_dot_nt(jnp.concatenate(ps, axis=0), v_t))
    return _unstack_heads(outs[0], outs[1], tq)


def _win_prompt_kernel(q_ref, k_ref, v_ref, kaux_ref, o_ref, s_ref, p_ref, *, tq, span):
    i = pl.program_id(1)
    qpos0 = i * tq
    k0 = pl.multiple_of(jnp.maximum(qpos0 + tq - span, 0), tq)
    qs = _stack_heads(q_ref[...])
    lane = _lane_iota((tq, LANES))
    k_aug = jnp.concatenate([k_ref[0, :, pl.ds(k0, span)], kaux_ref[:, pl.ds(k0, span)]], axis=0)
    v_t = v_ref[0, :, pl.ds(k0, span)]
    v_aug = jnp.concatenate([v_t, jnp.ones_like(v_t)], axis=0)
    row_minus_lane = _row_iota((FLASH_ROWS, LANES)) - _lane_iota((FLASH_ROWS, LANES))
    n_cols = span // LANES

    def attend(interior):
        outs = []
        for g in range(NSA_KV_HEADS):
            aux = jnp.concatenate(
                [jnp.where(lane == AUX_BLOCK_HI, _alibi_slope(g, r) * SEL_BLOCK,
                           jnp.where(lane == AUX_BLOCK_LO, _alibi_slope(g, r), 0.0)).astype(BF16)
                 for r in range(NSA_GROUP)], axis=0)
            s_ref[...] = _dot(jnp.concatenate([_group_lanes(qs, g), aux], axis=1), k_aug)
            for c in range(NSA_GROUP * tq // FLASH_ROWS):
                rows = slice(c * FLASH_ROWS, (c + 1) * FLASH_ROWS)
                dist0 = (qpos0 - k0) + (c * FLASH_ROWS) % tq + row_minus_lane
                parts = []
                for j in range(n_cols):
                    sj = s_ref[rows, j * LANES:(j + 1) * LANES]
                    dist = dist0 - j * LANES
                    if not interior or j == n_cols - 1:
                        sj = jnp.where(dist >= 0, sj, NEG_INF)
                    if interior and j == 0:
                        sj = jnp.where(dist <= WINDOW, sj, NEG_INF)
                    parts.append(sj)
                m = functools.reduce(jnp.maximum, parts).max(axis=-1, keepdims=True)
                for j in range(n_cols):
                    p_ref[rows, j * LANES:(j + 1) * LANES] = jnp.exp(parts[j] - m).astype(BF16)
            acc = _dot_nt(p_ref[...], v_aug)
            outs.append(acc[:, :LANES] / jnp.maximum(acc[:, LANES:], TINY))
        o_ref[...] = _unstack_heads(outs[0], outs[1], tq)

    first_interior = WINDOW // tq if span == tq + WINDOW else pl.num_programs(1)

    @pl.when(i >= first_interior)
    def _():
        attend(True)

    @pl.when(i < first_interior)
    def _():
        attend(False)


def _win_prompt(q2d, k_t, v_t, nb, t, tq):
    nt = t // tq
    span = min(tq + WINDOW, t)
    assert span % LANES == 0 and span % tq == 0 and tq % FLASH_ROWS == 0
    return pl.pallas_call(
        functools.partial(_win_prompt_kernel, tq=tq, span=span),
        grid=(nb, nt),
        in_specs=[pl.BlockSpec((tq, D_QA), lambda b, i: (b * nt + i, 0)),
                  pl.BlockSpec((1, D_KVA, t), lambda b, i: (b, 0, 0)),
                  pl.BlockSpec((1, D_KVA, t), lambda b, i: (b, 0, 0)),
                  _resident((LANES, t))],
        out_specs=pl.BlockSpec((tq, D_QA), lambda b, i: (b * nt + i, 0)),
        out_shape=jax.ShapeDtypeStruct((nb * t, D_QA), F32),
        scratch_shapes=[pltpu.VMEM((NSA_GROUP * tq, span), F32), pltpu.VMEM((NSA_GROUP * tq, span), BF16)],
        compiler_params=_params("parallel", "parallel"),
        name="win_prompt",
    )(q2d, k_t, v_t, _sel_key_features(t))


def _split3(x):
    a = x.astype(BF16)
    r = x - a.astype(F32)
    b = r.astype(BF16)
    c = (r - b.astype(F32)).astype(BF16)
    return a, b, c


def _dot3(x, m):
    a, b, c = _split3(x)
    return _dot(a, m) + (_dot(b, m) + _dot(c, m))


def _fox_bias_kernel(lf_ref, o_ref, *, t):
    upto = jnp.where(_row_iota((LANES, LANES)) <= _lane_iota((LANES, LANES)), 1.0, 0.0).astype(BF16)
    carry = jnp.zeros((FOX_HEADS, 1), F32)
    pad = jnp.zeros((LANES - 3 * FOX_HEADS, LANES), F32)
    for j in range(t // LANES):
        cols = slice(j * LANES, (j + 1) * LANES)
        c = _dot3(lf_ref[0, :, cols], upto) + carry
        carry = c[:, LANES - 1:LANES]
        terms = [term.astype(F32) for term in _split3(-c)]
        o_ref[0, :, cols] = jnp.concatenate(terms + [pad], axis=0).astype(BF16)


def _fox_bias(logf_t, nb, t):
    return pl.pallas_call(
        functools.partial(_fox_bias_kernel, t=t),
        grid=(nb,),
        in_specs=[pl.BlockSpec((1, FOX_HEADS, t), lambda b: (b, 0, 0))],
        out_specs=pl.BlockSpec((1, LANES, t), lambda b: (b, 0, 0)),
        out_shape=jax.ShapeDtypeStruct((nb, LANES, t), BF16),
        compiler_params=_params("parallel"),
        name="fox_bias",
    )(logf_t)


def _fox_prompt_kernel(q_ref, k_ref, v_ref, b_ref, o_ref, m_ref, alpha_ref, acc_ref, s_ref, p_ref, *, tq):
    i = pl.program_id(1)
    lane = _lane_iota((tq, LANES))
    lo_lanes = lane < HEAD_DIM
    causal = _causal_chunk_mask(0, 0, tq, tq)
    for p in range(FOX_HEADS // 2):
        qp = q_ref[:, p * LANES:(p + 1) * LANES]
        zero = jnp.zeros_like(qp)
        ones = lambda h: jnp.where((lane < 3 * FOX_HEADS) & (lane % FOX_HEADS == h), 1.0, 0.0).astype(BF16)
        q2 = jnp.concatenate([jnp.concatenate([jnp.where(lo_lanes, qp, zero), ones(2 * p)], axis=1),
                              jnp.concatenate([jnp.where(lo_lanes, zero, qp), ones(2 * p + 1)], axis=1)], axis=0)
        _flash_init(m_ref, acc_ref)

        def tile(kt, mask):
            k0 = pl.multiple_of(kt * tq, tq)
            k_aug = jnp.concatenate([k_ref[0, p * LANES:(p + 1) * LANES, pl.ds(k0, tq)], b_ref[0, :, pl.ds(k0, tq)]],
                                    axis=0)
            s_ref[...] = _dot(q2, k_aug)
            _flash_update(s_ref, p_ref, mask, v_ref[0, p * LANES:(p + 1) * LANES, pl.ds(k0, tq)],
                          m_ref, alpha_ref, acc_ref)

        def body(kt, carry):
            tile(kt, None)
            return carry

        lax.fori_loop(0, i, body, 0)
        tile(i, causal)
        o = _flash_result(acc_ref)
        o_ref[:, p * LANES:(p + 1) * LANES] = jnp.where(lo_lanes, o[:tq], o[tq:])


def _fox_prompt(q2d, k_t, v_t, bias, nb, t, tq):
    nt = t // tq
    return pl.pallas_call(
        functools.partial(_fox_prompt_kernel, tq=tq),
        grid=(nb, nt),
        in_specs=[pl.BlockSpec((tq, D_FOX), lambda b, i: (b * nt + i, 0)),
                  pl.BlockSpec((1, D_FOX, t), lambda b, i: (b, 0, 0)),
                  pl.BlockSpec((1, D_FOX, t), lambda b, i: (b, 0, 0)),
                  pl.BlockSpec((1, LANES, t), lambda b, i: (b, 0, 0))],
        out_specs=pl.BlockSpec((tq, D_FOX), lambda b, i: (b * nt + i, 0)),
        out_shape=jax.ShapeDtypeStruct((nb * t, D_FOX), F32),
        scratch_shapes=_flash_scratch(2 * tq, tq),
        compiler_params=_params("parallel", "parallel"),
        name="fox_prompt",
    )(q2d, k_t, v_t, bias)


def _merge_kernel(x_ref, oc_ref, os_ref, ow_ref, misc_ref, of_ref, ga_ref, gb_ref, woa_ref, wob_ref, wout_ref, h_ref):
    tm = x_ref.shape[0]
    misc = misc_ref[...]
    lo_lanes = _lane_iota((tm, LANES)) < HEAD_DIM

    def gate(head, j):
        c = 3 * head + j
        return jnp.broadcast_to(misc[:, c:c + 1], (tm, LANES))

    chunks = []
    for p in range(NSA_GROUP):
        cols = slice(p * LANES, (p + 1) * LANES)
        acc = None
        for j, o_ref in enumerate((oc_ref, os_ref, ow_ref)):
            term = jnp.where(lo_lanes, gate(p, j), gate(NSA_GROUP + p, j)) * o_ref[:, cols]
            acc = term if acc is None else acc + term
        chunks.append(acc.astype(BF16))
    ya = _dot(jnp.concatenate(chunks, axis=1), woa_ref[...])
    yb = _dot(of_ref[...].astype(BF16), wob_ref[...])
    mix = (ga_ref[...] * ya + gb_ref[...] * yb).astype(BF16)
    h_ref[...] = x_ref[...] + _dot(mix, wout_ref[...])


def _merge(x2d, o_cmp, o_sel, o_win, misc, o_fox, ga, gb, wts, tm):
    n = x2d.shape[0]
    assert n % tm == 0
    row = lambda w: pl.BlockSpec((tm, w), lambda i: (i, 0))
    par = [wts['w_o_a'], wts['w_o_b'], wts['w_out']]
    return pl.pallas_call(
        _merge_kernel,
        grid=(n // tm,),
        in_specs=[row(D_MODEL), row(D_QA), row(D_QA), row(D_QA), row(LANES), row(D_FOX), row(D_MODEL), row(D_MODEL)]
        + [_resident(a.shape) for a in par],
        out_specs=row(D_MODEL),
        out_shape=jax.ShapeDtypeStruct((n, D_MODEL), F32),
        compiler_params=_params("parallel"),
        name="merge",
    )(x2d, o_cmp, o_sel, o_win, misc, o_fox, ga, gb, *par)


FFN_CHUNK = 512


def _ffn_kernel(h_ref, p1_ref, p2_ref, g_ref, wup_ref, wgate_ref, wdown_ref, cw_ref, cb_ref, y_ref, tail_ref,
                carry_ref, *, tm, t_seq, carry):
    i = pl.program_id(0)
    h = h_ref[...]
    ms = jnp.mean(h * h, axis=-1, keepdims=True)
    hn = (h * lax.rsqrt(ms + EPS) * g_ref[...]).astype(BF16)
    t_in_seq = (i * tm + _row_iota((tm, FFN_CHUNK))) % t_seq
    acc = jnp.zeros((tm, D_MODEL), F32)
    for f in range(D_FF // FFN_CHUNK):
        cols = slice(f * FFN_CHUNK, (f + 1) * FFN_CHUNK)
        u = _dot(hn, wup_ref[:, cols])
        gt = _dot(hn, wgate_ref[:, cols])
        r1 = pltpu.roll(u, 1, 0)
        r2 = pltpu.roll(u, 2, 0)
        if carry:
            prev = carry_ref[:, cols]
            first = (i * tm) % t_seq == 0
            prev = jnp.where(first, jnp.zeros_like(prev), prev)
            row = _row_iota((tm, FFN_CHUNK))
            r1 = jnp.where(row == 0, prev[7:8, :], r1)
            r2 = jnp.where(row == 0, prev[6:7, :], jnp.where(row == 1, prev[7:8, :], r2))
            carry_ref[:, cols] = u[tm - 8:tm, :]
        else:
            r1 = jnp.where(t_in_seq >= 1, r1, p1_ref[:, cols])
            r2 = jnp.where(t_in_seq >= 2, r2, p2_ref[:, cols])
        uc = cb_ref[:, cols] + cw_ref[0:1, cols] * r2 + cw_ref[1:2, cols] * r1 + cw_ref[2:3, cols] * u
        acc = acc + _dot((_gelu(uc) * gt).astype(BF16), wdown_ref[cols, :])
        tail_ref[:, cols] = u[tm - tail_ref.shape[0]:tm, :]
    y_ref[...] = h + acc


def _ffn(h2d, p1, p2, wts, tm, t_seq, tail_rows):
    n = h2d.shape[0]
    assert n % tm == 0 and tail_rows % 8 == 0
    carry = p1 is None
    assert (t_seq % tm == 0) if carry else (tm % t_seq == 0)
    row = lambda w: pl.BlockSpec((tm, w), lambda i: (i, 0))
    par = [wts['g_ffn'], wts['w_up'], wts['w_gate'], wts['w_down'], wts['conv_w'], wts['conv_b']]
    if carry:
        kern = lambda h_ref, *rest, **kw: _ffn_kernel(h_ref, None, None, *rest, **kw)
        acts, act_specs = [h2d], [row(D_MODEL)]
    else:
        kern = _ffn_kernel
        acts, act_specs = [h2d, p1, p2], [row(D_MODEL), row(D_FF), row(D_FF)]
    return pl.pallas_call(
        functools.partial(kern, tm=tm, t_seq=t_seq, carry=carry),
        grid=(n // tm,),
        in_specs=act_specs + [_resident(a.shape) for a in par],
        out_specs=[row(D_MODEL), pl.BlockSpec((tail_rows, D_FF), lambda i: (i, 0))],
        out_shape=[jax.ShapeDtypeStruct((n, D_MODEL), F32),
                   jax.ShapeDtypeStruct((n // tm * tail_rows, D_FF), F32)],
        scratch_shapes=[pltpu.VMEM((8, D_FF), F32)],
        compiler_params=_params("arbitrary"),
        name="ffn",
    )(*acts, *par)


def _prep_weights(p):
    tile = lambda g, n: jnp.tile(g.astype(F32), n)[None, :]
    col = lambda g, n: jnp.tile(g.astype(F32), n)[:, None]
    w_tok, w_feat = _permute_w_in(p['w_in'])
    return dict(
        g_attn=p['norm_attn_g'].astype(F32)[None, :],
        w_tok=w_tok,
        w_feat=w_feat,
        m64=_head_mean_matrix(),
        gq=tile(p['q_norm_a_g'], NSA_HEADS) * ATTN_SCALE,
        gfq=tile(p['q_norm_b_g'], FOX_HEADS) * ATTN_SCALE,
        gkc=tile(p['k_norm_cmp_g'], NSA_KV_HEADS),
        gsk_col=col(p['k_norm_sel_g'], NSA_KV_HEADS),
        gwk_col=col(p['k_norm_win_g'], NSA_KV_HEADS),
        gfk_col=col(p['k_norm_b_g'], FOX_HEADS),
        bf_col=p['b_forget'].astype(F32)[:, None],
        cmpk=_cmp_weights(p['cmp_pe_k'], p['cmp_w1_k'], p['cmp_b1_k'], p['cmp_w2_k']),
        cmpv=_cmp_weights(p['cmp_pe_v'], p['cmp_w1_v'], p['cmp_b1_v'], p['cmp_w2_v']),
        w_o_a=p['w_o_a'][_qa_perm()].astype(BF16),
        w_o_b=p['w_o_b'].astype(BF16),
        w_out=p['w_out'].astype(BF16),
        g_ffn=p['norm_ffn_g'].astype(F32)[None, :],
        w_up=p['w_up'].astype(BF16),
        w_gate=p['w_gate'].astype(BF16),
        w_down=p['w_down'].astype(BF16),
        conv_w=p['conv_w'].astype(F32),
        conv_b=p['conv_b'].astype(F32)[None, :],
    )


Q_PAD = 16


SEL_HALVES = PAGE_SIZE // SEL_BLOCK


def _sel_sample_kernel(pt_ref, ix_ref, q_ref, *refs, n_cached, past):
    kb, vb = refs[:N_SEL], refs[N_SEL:2 * N_SEL]
    knew_ref, vnew_ref, o_ref = refs[2 * N_SEL:]
    b, t, g = pl.program_id(0), pl.program_id(1), pl.program_id(2)
    nk = (N_SEL + 1) * PAGE_SIZE
    kcat = jnp.concatenate([r[0, 0] for r in kb] + [knew_ref[0, 0]], axis=1).astype(BF16)
    vcat = jnp.concatenate([r[0, 0] for r in vb] + [vnew_ref[0, 0]], axis=1).astype(BF16)
    lane = _lane_iota((1, nk))
    slot = lane // PAGE_SIZE
    col = lane % PAGE_SIZE
    kpos = col + past
    valid = slot == N_SEL
    for j in range(N_SEL):
        blk = ix_ref[b, g * (ix_ref.shape[1] // NSA_KV_HEADS) + t * N_SEL + j]
        here = slot == j
        kpos = jnp.where(here, col + (blk // SEL_HALVES) * PAGE_SIZE, kpos)
        valid = valid | (here & (blk < n_cached) & ((col // SEL_BLOCK) == (blk % SEL_HALVES)))
    qpos = past + t
    dist = qpos - kpos
    mask = valid & (dist >= 0)
    row = _row_iota((Q_PAD, 1))
    slope = jnp.where(row == 0, 0.5, jnp.where(row == 1, 0.25, jnp.where(row == 2, 0.125, 0.0625)))
    slope = slope * jnp.where(g == 0, 1.0, 2.0 ** -NSA_GROUP)
    s = _dot(q_ref[0, 0, 0], kcat)
    p = _softmax_rows(s - slope * dist.astype(F32), jnp.broadcast_to(mask, s.shape))
    o_ref[0, 0, 0] = _dot_nt(p.astype(BF16), vcat)


def _sel_sample(qsel, pool_k, pool_v, knew, vnew, page_table, ix, past):
    nb, nt = qsel.shape[:2]
    n_cached = past // SEL_BLOCK

    def page_spec(j):
        def imap(b, t, g, pt, ixr):
            blk = jnp.minimum(ixr[b, g * (nt * N_SEL) + t * N_SEL + j], n_cached - 1)
            return (pt[b, blk // SEL_HALVES], g, 0, 0)
        return pl.BlockSpec((1, 1, HEAD_DIM, PAGE_SIZE), imap)

    new_spec = pl.BlockSpec((1, 1, HEAD_DIM, PAGE_SIZE), lambda b, t, g, pt, ixr: (b, g, 0, 0))
    q_spec = pl.BlockSpec((1, 1, 1, Q_PAD, HEAD_DIM), lambda b, t, g, pt, ixr: (b, t, g, 0, 0))
    return pl.pallas_call(
        functools.partial(_sel_sample_kernel, n_cached=n_cached, past=past),
        grid_spec=pltpu.PrefetchScalarGridSpec(
            num_scalar_prefetch=2,
            grid=(nb, nt, NSA_KV_HEADS),
            in_specs=[q_spec] + [page_spec(j) for j in range(N_SEL)] * 2 + [new_spec, new_spec],
            out_specs=q_spec,
        ),
        out_shape=jax.ShapeDtypeStruct((nb, nt, NSA_KV_HEADS, Q_PAD, HEAD_DIM), F32),
        compiler_params=_params("parallel", "parallel", "parallel"),
        name="sel_sample",
    )(page_table, ix, qsel, *([pool_k] * N_SEL), *([pool_v] * N_SEL), knew, vnew)


def _win_sample_kernel(q_ref, k_ref, v_ref, o_ref, *, past, w_buf):
    o_ref[...] = _window_core(q_ref[...], k_ref[0].astype(BF16), v_ref[0].astype(BF16), past, past - w_buf, Q_PAD)


def _win_sample(qpad, k_all, v_all, past, w_buf):
    nb, _, nk = k_all.shape
    return pl.pallas_call(
        functools.partial(_win_sample_kernel, past=past, w_buf=w_buf),
        grid=(nb,),
        in_specs=[pl.BlockSpec((Q_PAD, D_QA), lambda b: (b, 0)),
                  pl.BlockSpec((1, D_KVA, nk), lambda b: (b, 0, 0)),
                  pl.BlockSpec((1, D_KVA, nk), lambda b: (b, 0, 0))],
        out_specs=pl.BlockSpec((Q_PAD, D_QA), lambda b: (b, 0)),
        out_shape=jax.ShapeDtypeStruct((nb * Q_PAD, D_QA), F32),
        compiler_params=_params("parallel"),
        name="win_sample",
    )(qpad, k_all, v_all)


def _fox_decode_kernel(pt_ref, q_ref, *refs, pps, n_new):
    kp, vp, lp = refs[:pps], refs[pps:2 * pps], refs[2 * pps:3 * pps]
    knew_ref, vnew_ref, lnew_ref, o_ref, m_ref, l_ref, acc_ref, carry_ref = refs[3 * pps:]
    j = pl.program_id(1)
    rows = n_new * FOX_HEADS

    @pl.when(j == 0)
    def _():
        m_ref[...] = jnp.full(m_ref.shape, NEG_INF, F32)
        l_ref[...] = jnp.zeros(l_ref.shape, F32)
        acc_ref[...] = jnp.zeros(acc_ref.shape, F32)
        carry_ref[...] = jnp.zeros(carry_ref.shape, F32)

    q = q_ref[...].astype(F32)
    head_of_lane = _lane_iota((FOX_HEADS, D_FOX)) // HEAD_DIM
    own = head_of_lane == _row_iota((FOX_HEADS, D_FOX))
    wq = jnp.concatenate([jnp.where(own, jnp.broadcast_to(q[t:t + 1, :], (FOX_HEADS, D_FOX)), 0.0)
                          for t in range(n_new)], axis=0).astype(BF16)

    def update(s, mask, v):
        lg = s if mask is None else jnp.where(mask, s, NEG_INF)
        m_old = m_ref[...]
        m_new = jnp.maximum(m_old, jnp.max(lg, axis=-1, keepdims=True))
        p = jnp.exp(lg - m_new)
        if mask is not None:
            p = jnp.where(mask, p, 0.0)
        alpha = jnp.exp(m_old - m_new)
        l_ref[...] = alpha * l_ref[...] + jnp.sum(p, axis=-1, keepdims=True)
        m_ref[...] = m_new
        acc_ref[...] = alpha * acc_ref[...] + _dot_nt(p.astype(BF16), v)

    after = jnp.where(_row_iota((PAGE_SIZE, PAGE_SIZE)) > _lane_iota((PAGE_SIZE, PAGE_SIZE)), 1.0, 0.0).astype(BF16)
    carry = carry_ref[...]
    s_parts = []
    for u in range(pps):
        lft = lp[u][0]
        suffix = _dot3(lft, after) + carry[:, 0:1]
        carry = carry + jnp.sum(lft, axis=-1, keepdims=True)
        bias = jnp.concatenate([suffix] * n_new, axis=0)
        s_parts.append(_dot(wq, kp[u][0].astype(BF16)) + bias)
    carry_ref[...] = carry
    vcat = jnp.concatenate([r[0].astype(BF16) for r in vp], axis=1)
    update(jnp.concatenate(s_parts, axis=1), None, vcat)

    @pl.when(j == pl.num_programs(1) - 1)
    def _():
        upto = jnp.where(_row_iota((PAGE_SIZE, PAGE_SIZE)) <= _lane_iota((PAGE_SIZE, PAGE_SIZE)), 1.0, 0.0)
        c_new = _dot3(lnew_ref[0], upto.astype(BF16))
        s_new = _dot(wq, knew_ref[0].astype(BF16)) - jnp.concatenate([c_new] * n_new, axis=0)
        mask = _lane_iota((rows, PAGE_SIZE)) <= (_row_iota((rows, PAGE_SIZE)) // FOX_HEADS)
        update(s_new, mask, vnew_ref[0].astype(BF16))
        o = acc_ref[...] / jnp.maximum(l_ref[...], TINY)
        outs = [jnp.sum(jnp.where(own, o[t * FOX_HEADS:(t + 1) * FOX_HEADS], 0.0), axis=0, keepdims=True)
                for t in range(n_new)]
        o_ref[...] = jnp.concatenate(outs + [jnp.zeros((Q_PAD - n_new, D_FOX), F32)], axis=0)


def _fox_decode(qpad, pool_k, pool_v, pool_lft, knew, vnew, lnew_t, page_table, n_new, pps):
    nb, n_pages = page_table.shape
    assert n_pages % pps == 0
    steps = n_pages // pps

    def page_spec(u, shape):
        return pl.BlockSpec((1,) + shape, lambda b, j, pt: (pt[b, n_pages - 1 - (j * pps + u)], 0, 0))

    per_b = lambda shape: pl.BlockSpec((1,) + shape, lambda b, j, pt: (b, 0, 0))
    rows = n_new * FOX_HEADS
    return pl.pallas_call(
        functools.partial(_fox_decode_kernel, pps=pps, n_new=n_new),
        grid_spec=pltpu.PrefetchScalarGridSpec(
            num_scalar_prefetch=1,
            grid=(nb, steps),
            in_specs=[pl.BlockSpec((Q_PAD, D_FOX), lambda b, j, pt: (b, 0))]
            + [page_spec(u, (D_FOX, PAGE_SIZE)) for u in range(pps)] * 2
            + [page_spec(u, (FOX_HEADS, PAGE_SIZE)) for u in range(pps)]
            + [per_b((D_FOX, PAGE_SIZE)), per_b((D_FOX, PAGE_SIZE)), per_b((FOX_HEADS, PAGE_SIZE))],
            out_specs=pl.BlockSpec((Q_PAD, D_FOX), lambda b, j, pt: (b, 0)),
            scratch_shapes=[pltpu.VMEM((rows, 1), F32), pltpu.VMEM((rows, 1), F32), pltpu.VMEM((rows, D_FOX), F32),
                            pltpu.VMEM((FOX_HEADS, 1), F32)],
        ),
        out_shape=jax.ShapeDtypeStruct((nb * Q_PAD, D_FOX), F32),
        compiler_params=_params("parallel", "arbitrary"),
        name="fox_decode",
    )(page_table, qpad, *([pool_k] * pps), *([pool_v] * pps), *([pool_lft] * pps), knew, vnew, lnew_t)


def _pad_rows(a, rows):
    return jnp.pad(a, ((0, 0), (0, rows - a.shape[1]), (0, 0)))


def _sample_layer(x, cache, page_table, wts):
    (c_cmp_k, c_cmp_v, c_sel_k, c_sel_v, c_fox_k, c_fox_v, c_fox_logf, s_win_k, s_win_v, s_conv) = cache
    nb, t, _ = x.shape
    n = nb * t
    n_pool = c_cmp_k.shape[0]
    n_pages = page_table.shape[1]
    past = n_pages * PAGE_SIZE
    x2d = x.reshape(n, D_MODEL)
    pr = _project(x2d, wts, 1, n, n)
    per_seq = lambda a: a.reshape(nb, t, a.shape[-1])
    qpad = _pad_rows(per_seq(pr['qa']), Q_PAD).reshape(nb * Q_PAD, D_QA)
    take = lambda o: o.reshape(nb, Q_PAD, -1)[:, :t].reshape(n, -1)
    new_t = lambda a: jnp.swapaxes(a[0].reshape(a.shape[1], nb, t), 0, 1)
    lane_pad = lambda a, w: jnp.pad(a, ((0, 0), (0, 0), (0, w - a.shape[2])))
    paged = lambda pool: jnp.moveaxis(pool, 1, -1)

    nc = (past + t) // CMP_STRIDE
    assert nc * CMP_STRIDE <= past
    lhk, lhv = _cmp_lh_paged(c_cmp_k.reshape(n_pool, PAGE_SIZE, D_KVA), c_cmp_v.reshape(n_pool, PAGE_SIZE, D_KVA),
                             page_table, wts['cmpk'], wts['cmpv'], min(16, n_pages))
    kc, vc = _cmp_finish(lhk, lhv, wts['cmpk'], wts['cmpv'], wts['m64'], wts['gkc'], nc)
    n_slc = -(-(past + t) // SEL_BLOCK)
    o_cmp, idx = _cmp_attend(qpad, kc, vc, nb, Q_PAD, Q_PAD, nc, n_slc, past, 'idx')

    ix = idx[:, :, :t, :N_SEL].reshape(nb, NSA_KV_HEADS * t * N_SEL)
    qsel = jnp.swapaxes(pr['qa'].reshape(nb, t, NSA_GROUP, NSA_KV_HEADS, HEAD_DIM), 2, 3)
    qsel = jnp.pad(qsel, ((0, 0), (0, 0), (0, 0), (0, Q_PAD - NSA_GROUP), (0, 0)))
    grouped = lambda a: lane_pad(new_t(a), PAGE_SIZE).reshape(nb, NSA_KV_HEADS, HEAD_DIM, PAGE_SIZE)
    o_sel = _sel_sample(qsel, paged(c_sel_k), paged(c_sel_v), grouped(pr['skT']), grouped(pr['svT']),
                        page_table, ix, past)
    o_sel = jnp.swapaxes(o_sel[:, :, :, :NSA_GROUP], 2, 3).reshape(n, D_QA)

    w_buf = s_win_k.shape[1]
    wk_all = jnp.concatenate([paged(s_win_k).reshape(nb, D_KVA, w_buf), new_t(pr['wkT'])], axis=2)
    wv_all = jnp.concatenate([paged(s_win_v).reshape(nb, D_KVA, w_buf), new_t(pr['wvT'])], axis=2)
    o_win = _win_sample(qpad, lane_pad(wk_all, w_buf + LANES), lane_pad(wv_all, w_buf + LANES), past, w_buf)

    fqpad = _pad_rows(per_seq(pr['fq']), Q_PAD).reshape(nb * Q_PAD, D_FOX)
    o_fox = _fox_decode(fqpad, paged(c_fox_k).reshape(n_pool, D_FOX, PAGE_SIZE),
                        paged(c_fox_v).reshape(n_pool, D_FOX, PAGE_SIZE), paged(c_fox_logf),
                        lane_pad(new_t(pr['fkT']), PAGE_SIZE), lane_pad(new_t(pr['fvT']), PAGE_SIZE),
                        lane_pad(new_t(pr['logfT']), PAGE_SIZE), page_table, t, min(16, n_pages))

    h = _merge(x2d, take(o_cmp), o_sel, take(o_win), pr['gates'], take(o_fox), pr['ga'], pr['gb'], wts, n)
    zeros = jnp.zeros((nb, D_FF), F32)
    prev = s_conv.astype(F32)
    p1 = jnp.stack([prev[:, 1]] + [zeros] * (t - 1), axis=1).reshape(n, D_FF)
    p2 = jnp.stack([prev[:, 0], prev[:, 1]] + [zeros] * (t - 2), axis=1).reshape(n, D_FF)
    y, u = _ffn(h, p1, p2, wts, n, t, n)
    conv_rows = u.reshape(nb, t, D_FF)[:, t - (CONV_W - 1):]
    rows = lambda a, heads: a[0].T.reshape(nb, t, heads, HEAD_DIM)
    win = lambda a: jnp.moveaxis(a[:, :, t:t + w_buf].reshape(nb, NSA_KV_HEADS, HEAD_DIM, w_buf), -1, 1)
    state = (rows(pr['ckT'], NSA_KV_HEADS), rows(pr['cvT'], NSA_KV_HEADS), rows(pr['skT'], NSA_KV_HEADS),
             rows(pr['svT'], NSA_KV_HEADS), rows(pr['fkT'], FOX_HEADS), rows(pr['fvT'], FOX_HEADS),
             pr['logfT'][0].T.reshape(nb, t, FOX_HEADS), win(wk_all), win(wv_all), conv_rows)
    return y.reshape(nb, t, D_MODEL), state


def _tile_rows(n, pref):
    return pref if n % pref == 0 else n


def _prompt_layer(x, wts):
    nb, t, _ = x.shape
    n = nb * t
    x2d = x.reshape(n, D_MODEL)
    pr = _project(x2d, wts, nb, t, _tile_rows(t, 256))
    nc = t // CMP_STRIDE
    lhk, lhv = _cmp_lh_dense(pr['ck'].reshape(n // CMP_STRIDE, CHUNK_LANES), pr['cv'].reshape(n // CMP_STRIDE, CHUNK_LANES),
                             wts['cmpk'], wts['cmpv'], _tile_rows(n // CMP_STRIDE, 256))
    kc, vc = _cmp_finish(lhk, lhv, wts['cmpk'], wts['cmpv'], wts['m64'], wts['gkc'], nc)
    n_slc = -(-t // SEL_BLOCK)
    o_cmp, qaux = _cmp_attend(pr['qa'], kc, vc, nb, t, _tile_rows(t, 256), nc, n_slc, 0, 'features')
    o_sel = _sel_prompt(pr['qa'], pr['skTb'], pr['svTb'], qaux, nb, t, _tile_rows(t, 256), min(512, t))
    o_win = _win_prompt(pr['qa'], pr['wkTb'], pr['wvTb'], nb, t, 128)
    bias = _fox_bias(pr['logfT'], nb, t)
    o_fox = _fox_prompt(pr['fq'], pr['fkTb'], pr['fvTb'], bias, nb, t, min(512, t))
    h = _merge(x2d, o_cmp, o_sel, o_win, pr['gates'], o_fox, pr['ga'], pr['gb'], wts, _tile_rows(n, 512))
    tm = _tile_rows(t, 256)
    y, tail = _ffn(h, None, None, wts, tm, t, 8)
    conv_rows = tail.reshape(nb, t // tm, 8, D_FF)[:, -1, 8 - (CONV_W - 1):]
    rows = lambda a, heads: jnp.moveaxis(a.reshape(nb, heads, HEAD_DIM, a.shape[-1]), -1, 1)
    w_keep = min(WINDOW, t)
    state = (rows(pr['ckT'], NSA_KV_HEADS), rows(pr['cvT'], NSA_KV_HEADS), rows(pr['skT'], NSA_KV_HEADS),
             rows(pr['svT'], NSA_KV_HEADS), rows(pr['fkT'], FOX_HEADS), rows(pr['fvT'], FOX_HEADS),
             jnp.swapaxes(pr['logfT'], 1, 2), rows(pr['wkT'][:, :, t - w_keep:], NSA_KV_HEADS),
             rows(pr['wvT'][:, :, t - w_keep:], NSA_KV_HEADS), conv_rows)
    return y.reshape(nb, t, D_MODEL), state


_WEIGHT_NAMES = ('norm_attn_g', 'w_in', 'b_forget', 'q_norm_a_g', 'k_norm_cmp_g', 'k_norm_sel_g', 'k_norm_win_g',
                 'cmp_pe_k', 'cmp_w1_k', 'cmp_b1_k', 'cmp_w2_k', 'cmp_pe_v', 'cmp_w1_v', 'cmp_b1_v', 'cmp_w2_v',
                 'q_norm_b_g', 'k_norm_b_g', 'w_o_a', 'w_o_b', 'w_out', 'norm_ffn_g', 'w_up', 'w_gate', 'conv_w',
                 'conv_b', 'w_down')


def kernel(x_prompt, x_sample, cache_cmp_k, cache_cmp_v, cache_sel_k, cache_sel_v, cache_fox_k, cache_fox_v,
           cache_fox_logf, state_win_k, state_win_v, state_ffn_conv, page_table,
           norm_attn_g, w_in, b_forget, q_norm_a_g, k_norm_cmp_g, k_norm_sel_g, k_norm_win_g,
           cmp_pe_k, cmp_w1_k, cmp_b1_k, cmp_w2_k, cmp_pe_v, cmp_w1_v, cmp_b1_v, cmp_w2_v,
           q_norm_b_g, k_norm_b_g, w_o_a, w_o_b, w_out, norm_ffn_g, w_up, w_gate, conv_w, conv_b, w_down):
    weights = (norm_attn_g, w_in, b_forget, q_norm_a_g, k_norm_cmp_g, k_norm_sel_g, k_norm_win_g,
               cmp_pe_k, cmp_w1_k, cmp_b1_k, cmp_w2_k, cmp_pe_v, cmp_w1_v, cmp_b1_v, cmp_w2_v,
               q_norm_b_g, k_norm_b_g, w_o_a, w_o_b, w_out, norm_ffn_g, w_up, w_gate, conv_w, conv_b, w_down)
    caches = (cache_cmp_k, cache_cmp_v, cache_sel_k, cache_sel_v, cache_fox_k, cache_fox_v, cache_fox_logf,
              state_win_k, state_win_v, state_ffn_conv)
    depth = w_in.shape[0]
    prompt_states, sample_states = [], []
    for layer in range(depth):
        wts = _prep_weights({name: w[layer] for name, w in zip(_WEIGHT_NAMES, weights)})
        x_prompt, st_p = _prompt_layer(x_prompt, wts)
        x_sample, st_s = _sample_layer(x_sample, tuple(c[layer] for c in caches), page_table, wts)
        prompt_states.append(st_p)
        sample_states.append(st_s)
    new_prompt = [jnp.stack(rows) for rows in zip(*prompt_states)]
    new_sample = [jnp.stack(rows) for rows in zip(*sample_states)]
    return (x_prompt, x_sample, *new_prompt, *new_sample)
```
